```python
import jax, jax.numpy as jnp
from jax import lax
import numpy as np

D_MODEL = 4096
BATCH = 2
SEQ = 8192
DEPTH = 4

GRID_W = 64
CTX_LEN = 256
N_EVEN = (DEPTH + 1) // 2
N_ODD = DEPTH // 2
MOD_RANK = 512
N_MOD = 6
FFN_HIDDEN = -(-8 * D_MODEL // (3 * 256)) * 256
EPS = 1e-6
CHUNK = 128
A_WIDTH = D_MODEL // 2
A_GROUP_DIM = 128
A_GROUPS = A_WIDTH // A_GROUP_DIM
HEAD_DIM = 128
ATT_HEADS = (D_MODEL // 2) // HEAD_DIM
KV_HEADS = ATT_HEADS // 4
Q_BLOCK = 128
ROPE_THETA = 10000.0
EVEN_IN = 2 * A_WIDTH + (ATT_HEADS + 2 * KV_HEADS) * HEAD_DIM
EVEN_OUT = A_WIDTH + ATT_HEADS * HEAD_DIM
RET_HEADS = D_MODEL // 256
RET_QK_DIM = D_MODEL // RET_HEADS
RET_V_DIM = D_MODEL // RET_HEADS
RET_CHUNK = 128
RET_OUT = RET_HEADS * RET_V_DIM
RET_IN = 2 * RET_HEADS * RET_QK_DIM + 2 * RET_OUT

kernel_name = "hybrid_sgu_gqa_retention_dit"


def _rms(x):
    xf = x.astype(jnp.float32)
    return xf * lax.rsqrt(jnp.mean(xf * xf, axis=-1, keepdims=True) + EPS)


def rms_norm(x, gain):
    return (_rms(x) * gain.astype(jnp.float32)).astype(x.dtype)


def _modulate(xn, shift, scale):
    return xn * (1 + scale[:, None, :]) + shift[:, None, :]


def _adaln(cvec, down, up, bias):
    m = (jax.nn.silu(cvec) @ down) @ up + bias
    return jnp.split(m, N_MOD, axis=-1)


def _swiglu(h, w_gate, w_up, w_down):
    return (jax.nn.silu(h @ w_gate) * (h @ w_up)) @ w_down


def _axial_rope_tables(rows, head_dim):
    row = jnp.repeat(jnp.arange(rows), GRID_W).astype(jnp.float32)
    col = jnp.tile(jnp.arange(GRID_W), rows).astype(jnp.float32)
    n_freq = head_dim // 4
    inv_freq = ROPE_THETA ** (-jnp.arange(n_freq, dtype=jnp.float32) / n_freq)
    ang = jnp.concatenate([row[:, None] * inv_freq, col[:, None] * inv_freq], axis=-1)
    return jnp.cos(ang), jnp.sin(ang)


def _apply_rope(x, cos, sin):
    xf = x.astype(jnp.float32).reshape(*x.shape[:-1], -1, 2)
    x0, x1 = xf[..., 0], xf[..., 1]
    c = cos[:, None, :]
    s = sin[:, None, :]
    out = jnp.stack([x0 * c - x1 * s, x0 * s + x1 * c], axis=-1)
    return out.reshape(x.shape).astype(x.dtype)


def _gqa(q, k, v):
    b, nq, h, hd = q.shape
    qg = q.reshape(b, nq, KV_HEADS, h // KV_HEADS, hd)
    s = jnp.einsum('bqhgd,bkhd->bhgqk', qg, k, preferred_element_type=jnp.float32) * (hd ** -0.5)
    p = jax.nn.softmax(s, axis=-1).astype(v.dtype)
    o = jnp.einsum('bhgqk,bkhd->bqhgd', p, v)
    return o.reshape(b, nq, h * hd)


def _gqa_blocked(q, k, v):
    b, n, h, hd = q.shape
    qb = q.reshape(b, n // Q_BLOCK, Q_BLOCK, h, hd).transpose(1, 0, 2, 3, 4)
    ob = lax.map(lambda qi: _gqa(qi, k, v), qb)
    return ob.transpose(1, 0, 2, 3).reshape(b, n, h * hd)


def _chunk_sgu(z, norm_g, w_s, b_s):
    z = jax.nn.gelu(z)
    u, v = jnp.split(z, 2, axis=-1)
    b, n, _ = v.shape
    vc = rms_norm(v, norm_g).reshape(b, n // CHUNK, CHUNK, A_GROUPS, A_GROUP_DIM)
    mixed = jnp.einsum('gts,bcsgd->bctgd', w_s, vc) + b_s.T[None, None, :, :, None]
    return u * mixed.reshape(b, n, A_WIDTH)


def _even_split(p):
    b, n, _ = p.shape
    q0 = 2 * A_WIDTH
    k0 = q0 + ATT_HEADS * HEAD_DIM
    v0 = k0 + KV_HEADS * HEAD_DIM
    z, q, k, v = jnp.split(p, [q0, k0, v0], axis=-1)
    return (z, q.reshape(b, n, ATT_HEADS, HEAD_DIM),
            k.reshape(b, n, KV_HEADS, HEAD_DIM), v.reshape(b, n, KV_HEADS, HEAD_DIM))


def _even_mixer(h_ctx, h_lat, w_in, w_out, sgu_norm, sgu_w, sgu_b, q_norm, k_norm,
                cos, sin, with_ctx_out):
    z_c, q_c, k_c, v_c = _even_split(h_ctx @ w_in)
    z_l, q_l, k_l, v_l = _even_split(h_lat @ w_in)
    k_c = rms_norm(k_c, k_norm)
    q_l = _apply_rope(rms_norm(q_l, q_norm), cos, sin)
    k_l = _apply_rope(rms_norm(k_l, k_norm), cos, sin)
    k_all = jnp.concatenate([k_c, k_l], axis=1)
    v_all = jnp.concatenate([v_c, v_l], axis=1)
    y_lat = jnp.concatenate([_chunk_sgu(z_l, sgu_norm, sgu_w, sgu_b),
                             _gqa_blocked(q_l, k_all, v_all)], axis=-1) @ w_out
    y_ctx = None
    if with_ctx_out:
        y_ctx = jnp.concatenate([_chunk_sgu(z_c, sgu_norm, sgu_w, sgu_b),
                                 _gqa(rms_norm(q_c, q_norm), k_c, v_c)], axis=-1) @ w_out
    return y_ctx, y_lat


def _retention_chunks(q, k, v, log_decay, s0, include_diag):
    b, h, n, _ = q.shape
    dv = v.shape[-1]
    idx = jnp.arange(RET_CHUNK, dtype=jnp.float32)
    dist = idx[:, None] - idx[None, :]
    mask = (dist >= 0) if include_diag else (dist > 0)
    inner_decay = jnp.where(mask[None], jnp.exp(log_decay[:, None, None] * jnp.maximum(dist, 0.0)[None]), 0.0)
    q_decay = jnp.exp(log_decay[:, None] * (idx + 1.0))[:, :, None]
    k_decay = jnp.exp(log_decay[:, None] * (RET_CHUNK - 1.0 - idx))[:, :, None]
    chunk_decay = jnp.exp(log_decay * RET_CHUNK)[:, None, None]

    def to_chunks(t):
        return t.reshape(b, h, n // RET_CHUNK, RET_CHUNK, t.shape[-1]).transpose(2, 0, 1, 3, 4)

    def step(s, qkv):
        qc, kc, vc = qkv
        att = jnp.einsum('bhqd,bhkd->bhqk', qc, kc) * inner_decay
        o = (jnp.einsum('bhqk,bhkv->bhqv', att, vc)
             + jnp.einsum('bhqd,bhdv->bhqv', qc * q_decay, s))
        s = s * chunk_decay + jnp.einsum('bhkd,bhkv->bhdv', kc * k_decay, vc)
        return s, o

    s_final, o = lax.scan(step, s0, (to_chunks(q), to_chunks(k), to_chunks(v)))
    return o.transpose(1, 2, 0, 3, 4).reshape(b, h, n, dv), s_final


def _ret_split(p, cos=None, sin=None):
    b, n, _ = p.shape
    qk_w = RET_HEADS * RET_QK_DIM
    q, k, v, g = jnp.split(p, [qk_w, 2 * qk_w, 2 * qk_w + RET_OUT], axis=-1)
    q = q.reshape(b, n, RET_HEADS, RET_QK_DIM)
    k = k.reshape(b, n, RET_HEADS, RET_QK_DIM) * (RET_QK_DIM ** -0.5)
    if cos is not None:
        q = _apply_rope(q, cos, sin)
        k = _apply_rope(k, cos, sin)
    v = v.reshape(b, n, RET_HEADS, RET_V_DIM)

    def heads(t):
        return t.astype(jnp.float32).transpose(0, 2, 1, 3)

    return heads(q), heads(k), heads(v), g


def _retention_out(o, g, w_out):
    b, h, n, dv = o.shape
    o = _rms(o).transpose(0, 2, 1, 3).reshape(b, n, h * dv).astype(g.dtype)
    return (jax.nn.silu(g) * o) @ w_out


def _odd_mixer(h_ctx, h_lat, w_in, w_out, decay_fwd, decay_bwd, cos, sin, with_ctx_out):
    q_c, k_c, v_c, g_c = _ret_split(h_ctx @ w_in)
    q_l, k_l, v_l, g_l = _ret_split(h_lat @ w_in, cos, sin)
    log_f = -jnp.exp(decay_fwd.astype(jnp.float32))
    log_b = -jnp.exp(decay_bwd.astype(jnp.float32))
    s0 = jnp.zeros((h_ctx.shape[0], RET_HEADS, RET_QK_DIM, RET_V_DIM), jnp.float32)

    def flip(t):
        return jnp.flip(t, axis=2)

    o_cf, s_cf = _retention_chunks(q_c, k_c, v_c, log_f, s0, True)
    o_cb, s_cb = _retention_chunks(flip(q_c), flip(k_c), flip(v_c), log_b, s0, False)
    o_lf, _ = _retention_chunks(q_l, k_l, v_l, log_f, s_cf, True)
    o_lb, _ = _retention_chunks(flip(q_l), flip(k_l), flip(v_l), log_b, s_cb, False)
    y_lat = _retention_out(o_lf + flip(o_lb), g_l, w_out)
    y_ctx = _retention_out(o_cf + flip(o_cb), g_c, w_out) if with_ctx_out else None
    return y_ctx, y_lat


def setup_inputs(seed: int = 0) -> dict:
    key = jax.random.key(seed)
    ks = jax.random.split(key, 23)
    f32 = jnp.float32

    def nrm(k, shape, scale):
        return jax.random.normal(k, shape, f32) * scale

    base_decay = jnp.log(-jnp.log(1.0 - 2.0 ** (-5.0 - jnp.arange(RET_HEADS, dtype=f32))))
    return {
        "x": nrm(ks[0], (BATCH, SEQ, D_MODEL), 1.0),
        "c": nrm(ks[1], (BATCH, D_MODEL), 1.0),
        "ctx": nrm(ks[2], (BATCH, CTX_LEN, D_MODEL), 1.0),
        "c_ctx": nrm(ks[3], (D_MODEL,), 1.0),
        "norm_mix": 1.0 + nrm(ks[4], (DEPTH, D_MODEL), 0.02),
        "norm_ffn": 1.0 + nrm(ks[5], (DEPTH, D_MODEL), 0.02),
        "mod_down": nrm(ks[6], (DEPTH, D_MODEL, MOD_RANK), D_MODEL ** -0.5),
        "mod_up": nrm(ks[7], (DEPTH, MOD_RANK, N_MOD * D_MODEL), 0.5 * MOD_RANK ** -0.5),
        "mod_bias": nrm(ks[8], (DEPTH, N_MOD * D_MODEL), 0.01),
        "ffn_gate": nrm(ks[9], (DEPTH, D_MODEL, FFN_HIDDEN), D_MODEL ** -0.5),
        "ffn_up": nrm(ks[10], (DEPTH, D_MODEL, FFN_HIDDEN), D_MODEL ** -0.5),
        "ffn_down": nrm(ks[11], (DEPTH, FFN_HIDDEN, D_MODEL), FFN_HIDDEN ** -0.5),
        "ev_w_in": nrm(ks[12], (N_EVEN, D_MODEL, EVEN_IN), D_MODEL ** -0.5),
        "ev_w_out": nrm(ks[13], (N_EVEN, EVEN_OUT, D_MODEL), EVEN_OUT ** -0.5),
        "sgu_norm": 1.0 + nrm(ks[14], (N_EVEN, A_WIDTH), 0.02),
        "sgu_w": nrm(ks[15], (N_EVEN, A_GROUPS, CHUNK, CHUNK), 0.5 * CHUNK ** -0.5),
        "sgu_b": 1.0 + nrm(ks[16], (N_EVEN, A_GROUPS, CHUNK), 0.01),
        "q_norm": 1.0 + nrm(ks[17], (N_EVEN, HEAD_DIM), 0.02),
        "k_norm": 1.0 + nrm(ks[18], (N_EVEN, HEAD_DIM), 0.02),
        "ret_w_in": nrm(ks[19], (N_ODD, D_MODEL, RET_IN), D_MODEL ** -0.5),
        "ret_w_out": nrm(ks[20], (N_ODD, RET_OUT, D_MODEL), RET_OUT ** -0.5),
        "ret_decay_fwd": base_decay[None] + nrm(ks[21], (N_ODD, RET_HEADS), 0.05),
        "ret_decay_bwd": base_decay[None] + nrm(ks[22], (N_ODD, RET_HEADS), 0.05),
    }


def reference(x, c, ctx, c_ctx, norm_mix, norm_ffn, mod_down, mod_up, mod_bias,
              ffn_gate, ffn_up, ffn_down, ev_w_in, ev_w_out, sgu_norm, sgu_w, sgu_b,
              q_norm, k_norm, ret_w_in, ret_w_out, ret_decay_fwd, ret_decay_bwd):
    n = x.shape[1]
    rows = n // GRID_W
    cos_a, sin_a = _axial_rope_tables(rows, HEAD_DIM)
    cos_r, sin_r = _axial_rope_tables(rows, RET_QK_DIM)
    h_lat = x
    h_ctx = ctx
    for i in range(DEPTH):
        last = i == DEPTH - 1
        sh1, sc1, g1, sh2, sc2, g2 = _adaln(c, mod_down[i], mod_up[i], mod_bias[i])
        csh1, csc1, cg1, csh2, csc2, cg2 = _adaln(c_ctx[None], mod_down[i], mod_up[i], mod_bias[i])
        xn = _modulate(rms_norm(h_lat, norm_mix[i]), sh1, sc1)
        cn = _modulate(rms_norm(h_ctx, norm_mix[i]), csh1, csc1)
        j = i // 2
        if i % 2 == 0:
            y_ctx, y_lat = _even_mixer(cn, xn, ev_w_in[j], ev_w_out[j], sgu_norm[j], sgu_w[j],
                                       sgu_b[j], q_norm[j], k_norm[j], cos_a, sin_a, not last)
        else:
            y_ctx, y_lat = _odd_mixer(cn, xn, ret_w_in[j], ret_w_out[j], ret_decay_fwd[j],
                                      ret_decay_bwd[j], cos_r, sin_r, not last)
        h_lat = h_lat + g1[:, None, :] * y_lat
        h_lat = h_lat + g2[:, None, :] * _swiglu(
            _modulate(rms_norm(h_lat, norm_ffn[i]), sh2, sc2), ffn_gate[i], ffn_up[i], ffn_down[i])
        if not last:
            h_ctx = h_ctx + cg1[:, None, :] * y_ctx
            h_ctx = h_ctx + cg2[:, None, :] * _swiglu(
                _modulate(rms_norm(h_ctx, norm_ffn[i]), csh2, csc2), ffn_gate[i], ffn_up[i], ffn_down[i])
    return h_lat
```

```python
import functools

import jax
import jax.numpy as jnp
from jax import lax
from jax.experimental import pallas as pl
from jax.experimental.pallas import tpu as pltpu

F32 = jnp.float32
BF16 = jnp.bfloat16

EPS = 1e-6
GRID_W = 64
N_MOD = 6
ROPE_THETA = 10000.0
SGU_CHUNK = 128
SGU_GROUP = 128
HEAD_DIM = 128
KV_HEADS = 4
RET_DIM = 256
RET_CHUNK = 256
LANES = 128
VMEM_LIMIT = 56 * 1024 * 1024


def _cparams(sem):
    return pltpu.CompilerParams(dimension_semantics=sem, vmem_limit_bytes=VMEM_LIMIT)


def _sigmoid(x):
    return 1.0 / (1.0 + jnp.exp(-x))


def _gelu(x):
    return 0.5 * x * (1.0 + jnp.tanh(0.7978845608028654 * (x + 0.044715 * (x * x * x))))


def _dot(a, b):
    return jnp.dot(a, b, preferred_element_type=F32)


def _dot_nt(a, b):
    return lax.dot_general(a, b, (((1,), (1,)), ((), ())), preferred_element_type=F32)


def _dot_tn(a, b):
    return lax.dot_general(a, b, (((0,), (0,)), ((), ())), preferred_element_type=F32)


def _adaln_kernel(cv_ref, down_ref, up_ref, bias_ref, o_ref, hd_ref):
    @pl.when(pl.program_id(1) == 0)
    def _():
        cv = cv_ref[...]
        hd_ref[...] = _dot(cv * _sigmoid(cv), down_ref[...])

    o_ref[...] = _dot(hd_ref[...], up_ref[...]) + bias_ref[...]


def adaln_all(cv, down, up, bias, tn=2048):
    depth, d, rank = down.shape
    nout = up.shape[-1]
    rows = cv.shape[0]
    return pl.pallas_call(
        _adaln_kernel,
        grid=(depth, nout // tn),
        in_specs=[
            pl.BlockSpec((rows, d), lambda l, j: (0, 0)),
            pl.BlockSpec((None, d, rank), lambda l, j: (l, 0, 0)),
            pl.BlockSpec((None, rank, tn), lambda l, j: (l, 0, j)),
            pl.BlockSpec((None, 1, tn), lambda l, j: (l, 0, j)),
        ],
        out_specs=pl.BlockSpec((None, rows, tn), lambda l, j: (l, 0, j)),
        out_shape=jax.ShapeDtypeStruct((depth, rows, nout), F32),
        scratch_shapes=[pltpu.VMEM((rows, rank), F32)],
        compiler_params=_cparams(("arbitrary", "arbitrary")),
        name="adaln",
    )(cv, down, up, bias.reshape(depth, 1, nout))


def _norm_mod_kernel(x_ref, gain_ref, scale_ref, shift_ref, o_ref):
    x = x_ref[...]
    ms = jnp.mean(x * x, axis=-1, keepdims=True)
    xn = x * lax.rsqrt(ms + EPS) * gain_ref[...]
    o_ref[...] = (xn * (1.0 + scale_ref[...]) + shift_ref[...]).astype(o_ref.dtype)


def norm_mod(x, gain, scale, shift, tm=512):
    m, d = x.shape
    nb = scale.shape[0]
    assert (m // tm) % nb == 0
    tpb = (m // tm) // nb
    vec = pl.BlockSpec((None, 1, d), lambda i: (i // tpb, 0, 0))
    return pl.pallas_call(
        _norm_mod_kernel,
        grid=(m // tm,),
        in_specs=[pl.BlockSpec((tm, d), lambda i: (i, 0)),
                  pl.BlockSpec((1, d), lambda i: (0, 0)), vec, vec],
        out_specs=pl.BlockSpec((tm, d), lambda i: (i, 0)),
        out_shape=jax.ShapeDtypeStruct((m, d), BF16),
        compiler_params=_cparams(("parallel",)),
        name="norm_mod",
    )(x, gain.reshape(1, d), scale, shift)


def _mm_store_kernel(a_ref, w_ref, o_ref):
    o_ref[...] = _dot(a_ref[...], w_ref[...]).astype(o_ref.dtype)


def matmul(a, w, out_dtype, tm, tn):
    m, k = a.shape
    n = w.shape[1]
    return pl.pallas_call(
        _mm_store_kernel,
        grid=(m // tm, n // tn),
        in_specs=[pl.BlockSpec((tm, k), lambda i, j: (i, 0)),
                  pl.BlockSpec((k, tn), lambda i, j: (0, j))],
        out_specs=pl.BlockSpec((tm, tn), lambda i, j: (i, j)),
        out_shape=jax.ShapeDtypeStruct((m, n), out_dtype),
        compiler_params=_cparams(("parallel", "arbitrary")),
        name="matmul",
    )(a, w)


def _mm_resid_kernel(*refs, n_a):
    a_refs, w_refs = refs[:n_a], refs[n_a:2 * n_a]
    resid_ref, gate_ref, o_ref = refs[2 * n_a:]
    acc = _dot(a_refs[0][...], w_refs[0][...])
    for a_ref, w_ref in zip(a_refs[1:], w_refs[1:]):
        acc = acc + _dot(a_ref[...], w_ref[...])
    o_ref[...] = resid_ref[...] + gate_ref[...] * acc


def matmul_resid(a_list, w, resid, gate, tm, tn):
    m, n = resid.shape
    nb = gate.shape[0]
    assert (m // tm) % nb == 0
    tpb = (m // tm) // nb
    n_a = len(a_list)
    in_specs, w_args, off = [], [], 0
    for a in a_list:
        in_specs.append(pl.BlockSpec((tm, a.shape[1]), lambda i, j: (i, 0)))
    for a in a_list:
        ka = a.shape[1]
        assert off % ka == 0
        in_specs.append(pl.BlockSpec((ka, tn), lambda i, j, r=off // ka: (r, j)))
        w_args.append(w)
        off += ka
    assert off == w.shape[0]
    in_specs += [pl.BlockSpec((tm, tn), lambda i, j: (i, j)),
                 pl.BlockSpec((None, 1, tn), lambda i, j: (i // tpb, 0, j))]
    return pl.pallas_call(
        functools.partial(_mm_resid_kernel, n_a=n_a),
        grid=(m // tm, n // tn),
        in_specs=in_specs,
        out_specs=pl.BlockSpec((tm, tn), lambda i, j: (i, j)),
        out_shape=jax.ShapeDtypeStruct((m, n), F32),
        compiler_params=_cparams(("parallel", "arbitrary")),
        name="matmul_resid",
    )(*a_list, *w_args, resid, gate)


def _mm_swiglu_kernel(a_ref, wg_ref, wu_ref, o_ref):
    a = a_ref[...]
    g = _dot(a, wg_ref[...])
    u = _dot(a, wu_ref[...])
    o_ref[...] = (g * _sigmoid(g) * u).astype(o_ref.dtype)


def matmul_swiglu(a, wg, wu, tm, tn):
    m, k = a.shape
    n = wg.shape[1]
    wspec = pl.BlockSpec((k, tn), lambda i, j: (0, j))
    return pl.pallas_call(
        _mm_swiglu_kernel,
        grid=(m // tm, n // tn),
        in_specs=[pl.BlockSpec((tm, k), lambda i, j: (i, 0)), wspec, wspec],
        out_specs=pl.BlockSpec((tm, tn), lambda i, j: (i, j)),
        out_shape=jax.ShapeDtypeStruct((m, n), BF16),
        compiler_params=_cparams(("parallel", "arbitrary")),
        name="matmul_swiglu",
    )(a, wg, wu)


def _rope_slab(x, cos, sin_signed, even):
    swapped = jnp.where(even, pltpu.roll(x, LANES - 1, 1), pltpu.roll(x, 1, 1))
    return x * cos + swapped * sin_signed


def _even_lanes(rows):
    return (lax.broadcasted_iota(jnp.int32, (rows, LANES), 1) % 2) == 0


def _sgu_kernel(u_ref, v_ref, gain_ref, w_ref, b_ref, o_ref):
    tm = u_ref.shape[0]
    groups = w_ref.shape[0]
    v = _gelu(v_ref[...].astype(F32))
    ms = jnp.mean(v * v, axis=-1, keepdims=True)
    vn = (v * lax.rsqrt(ms + EPS) * gain_ref[...]).astype(BF16)
    bias = b_ref[...]
    for c in range(tm // SGU_CHUNK):
        rows = slice(c * SGU_CHUNK, (c + 1) * SGU_CHUNK)
        for g in range(groups):
            cols = slice(g * SGU_GROUP, (g + 1) * SGU_GROUP)
            mixed = _dot(w_ref[g], vn[rows, cols]) + bias[:, g:g + 1]
            u = _gelu(u_ref[rows, cols].astype(F32))
            o_ref[rows, cols] = (u * mixed).astype(o_ref.dtype)


def sgu(p, gain, w_s, b_s, tm=256):
    m = p.shape[0]
    groups = w_s.shape[0]
    width = groups * SGU_GROUP
    return pl.pallas_call(
        _sgu_kernel,
        grid=(m // tm,),
        in_specs=[pl.BlockSpec((tm, width), lambda i: (i, 0)),
                  pl.BlockSpec((tm, width), lambda i: (i, 1)),
                  pl.BlockSpec((1, width), lambda i: (0, 0)),
                  pl.BlockSpec((groups, SGU_CHUNK, SGU_CHUNK), lambda i: (0, 0, 0)),
                  pl.BlockSpec((SGU_CHUNK, groups), lambda i: (0, 0))],
        out_specs=pl.BlockSpec((tm, width), lambda i: (i, 0)),
        out_shape=jax.ShapeDtypeStruct((m, width), BF16),
        compiler_params=_cparams(("parallel",)),
        name="sgu",
    )(p, p, gain.reshape(1, width), w_s.astype(BF16), b_s.T)


def _qkv_prep_kernel(q_ref, k_ref, v_ref, qg_ref, kg_ref, cos_ref, sin_ref,
                     qo_ref, ko_ref, vo_ref, *, rope, q_scale):
    tm = q_ref.shape[0]
    even = _even_lanes(tm)
    cos = cos_ref[...]
    sin = sin_ref[...]

    def prep(x, gain):
        ms = jnp.mean(x * x, axis=-1, keepdims=True)
        xn = x * lax.rsqrt(ms + EPS) * gain
        if rope:
            xn = _rope_slab(xn, cos, sin, even)
        return xn

    qg = qg_ref[...]
    kg = kg_ref[...]
    for h in range(q_ref.shape[1] // HEAD_DIM):
        cols = slice(h * HEAD_DIM, (h + 1) * HEAD_DIM)
        qo_ref[:, cols] = (prep(q_ref[:, cols].astype(F32), qg) * q_scale).astype(qo_ref.dtype)
    for h in range(k_ref.shape[1] // HEAD_DIM):
        cols = slice(h * HEAD_DIM, (h + 1) * HEAD_DIM)
        ko_ref[:, cols] = prep(k_ref[:, cols].astype(F32), kg).astype(ko_ref.dtype)
    vo_ref[...] = v_ref[...].astype(vo_ref.dtype)


def qkv_prep(p, q_gain, k_gain, cos, sin, col0, n_q, n_kv, rope, tm=256):
    m = p.shape[0]
    qw, kw = n_q * HEAD_DIM, n_kv * HEAD_DIM
    assert col0 % qw == 0 and (col0 + qw) % kw == 0
    pos_tiles = cos.shape[0] // tm
    tab = pl.BlockSpec((tm, HEAD_DIM), lambda i: (i % pos_tiles, 0))
    kern = functools.partial(_qkv_prep_kernel, rope=rope, q_scale=HEAD_DIM ** -0.5)
    return pl.pallas_call(
        kern,
        grid=(m // tm,),
        in_specs=[pl.BlockSpec((tm, qw), lambda i: (i, col0 // qw)),
                  pl.BlockSpec((tm, kw), lambda i: (i, (col0 + qw) // kw)),
                  pl.BlockSpec((tm, kw), lambda i: (i, (col0 + qw) // kw + 1)),
                  pl.BlockSpec((1, HEAD_DIM), lambda i: (0, 0)),
                  pl.BlockSpec((1, HEAD_DIM), lambda i: (0, 0)),
                  tab, tab],
        out_specs=[pl.BlockSpec((tm, qw), lambda i: (i, 0)),
                   pl.BlockSpec((tm, kw), lambda i: (i, 0)),
                   pl.BlockSpec((tm, kw), lambda i: (i, 0))],
        out_shape=[jax.ShapeDtypeStruct((m, qw), BF16),
                   jax.ShapeDtypeStruct((m, kw), BF16),
                   jax.ShapeDtypeStruct((m, kw), BF16)],
        compiler_params=_cparams(("parallel",)),
        name="qkv_prep",
    )(p, p, p, q_gain.reshape(1, HEAD_DIM), k_gain.reshape(1, HEAD_DIM), cos, sin)


def _flash_kernel(q_ref, k_ref, v_ref, o_ref, m_ref, l_ref, acc_ref, *, tk, group):
    tq = q_ref.shape[0]
    nk = k_ref.shape[0] // tk
    q = jnp.concatenate(
        [q_ref[:, g * HEAD_DIM:(g + 1) * HEAD_DIM] for g in range(group)], axis=0)
    m_ref[...] = jnp.full(m_ref.shape, -jnp.inf, F32)
    l_ref[...] = jnp.zeros(l_ref.shape, F32)
    acc_ref[...] = jnp.zeros(acc_ref.shape, F32)

    def body(j, carry):
        off = pl.multiple_of(j * tk, tk)
        k = k_ref[pl.ds(off, tk), :]
        v = v_ref[pl.ds(off, tk), :]
        s = _dot_nt(q, k)
        m_prev = m_ref[...]
        m_new = jnp.maximum(m_prev, jnp.max(s, axis=-1, keepdims=True))
        alpha = jnp.exp(m_prev - m_new)
        p = jnp.exp(s - m_new)
        l_ref[...] = alpha * l_ref[...] + jnp.sum(p, axis=-1, keepdims=True)
        acc_ref[...] = alpha * acc_ref[...] + _dot(p.astype(BF16), v)
        m_ref[...] = m_new
        return carry

    lax.fori_loop(0, nk, body, 0)
    out = acc_ref[...] / l_ref[...]
    for g in range(group):
        o_ref[:, g * HEAD_DIM:(g + 1) * HEAD_DIM] = out[g * tq:(g + 1) * tq].astype(o_ref.dtype)


def flash_attention(q, k, v, tq, tk):
    b, n, qw = q.shape
    s = k.shape[1]
    kvh = k.shape[2] // HEAD_DIM
    group = qw // HEAD_DIM // kvh
    gw = group * HEAD_DIM
    assert s % tk == 0 and n % tq == 0
    return pl.pallas_call(
        functools.partial(_flash_kernel, tk=tk, group=group),
        grid=(b, kvh, n // tq),
        in_specs=[pl.BlockSpec((None, tq, gw), lambda bb, h, i: (bb, i, h)),
                  pl.BlockSpec((None, s, HEAD_DIM), lambda bb, h, i: (bb, 0, h)),
                  pl.BlockSpec((None, s, HEAD_DIM), lambda bb, h, i: (bb, 0, h))],
        out_specs=pl.BlockSpec((None, tq, gw), lambda bb, h, i: (bb, i, h)),
        out_shape=jax.ShapeDtypeStruct((b, n, qw), BF16),
        scratch_shapes=[pltpu.VMEM((group * tq, 1), F32),
                        pltpu.VMEM((group * tq, 1), F32),
                        pltpu.VMEM((group * tq, HEAD_DIM), F32)],
        compiler_params=_cparams(("parallel", "parallel", "arbitrary")),
        name="flash_attention",
    )(q, k, v)


def _ret_prep_kernel(q_ref, k_ref, v_ref, cos_ref, sin_ref, qo_ref, ko_ref, vo_ref,
                     *, rope, k_scale):
    tm = q_ref.shape[0]
    even = _even_lanes(tm)
    slabs_per_head = RET_DIM // LANES
    for sl in range(q_ref.shape[1] // LANES):
        cols = slice(sl * LANES, (sl + 1) * LANES)
        tcols = slice((sl % slabs_per_head) * LANES, (sl % slabs_per_head + 1) * LANES)
        q = q_ref[:, cols].astype(F32)
        k = k_ref[:, cols].astype(F32) * k_scale
        if rope:
            cos = cos_ref[:, tcols]
            sin = sin_ref[:, tcols]
            q = _rope_slab(q, cos, sin, even)
            k = _rope_slab(k, cos, sin, even)
        qo_ref[:, cols] = q.astype(qo_ref.dtype)
        ko_ref[:, cols] = k.astype(ko_ref.dtype)
    vo_ref[...] = v_ref[...].astype(vo_ref.dtype)


def ret_prep(p, cos, sin, width, rope, tm=256):
    m = p.shape[0]
    pos_tiles = cos.shape[0] // tm
    tab = pl.BlockSpec((tm, RET_DIM), lambda i: (i % pos_tiles, 0))
    out = jax.ShapeDtypeStruct((m, width), BF16)
    ospec = pl.BlockSpec((tm, width), lambda i: (i, 0))
    return pl.pallas_call(
        functools.partial(_ret_prep_kernel, rope=rope, k_scale=RET_DIM ** -0.5),
        grid=(m // tm,),
        in_specs=[pl.BlockSpec((tm, width), lambda i: (i, 0)),
                  pl.BlockSpec((tm, width), lambda i: (i, 1)),
                  pl.BlockSpec((tm, width), lambda i: (i, 2)),
                  tab, tab],
        out_specs=[ospec, ospec, ospec],
        out_shape=[out, out, out],
        compiler_params=_cparams(("parallel",)),
        name="ret_prep",
    )(p, p, p, cos, sin)


def _ret_kernel(dec_ref, qf_ref, kf_ref, vf_ref, qb_ref, kb_ref, vb_ref, s0_ref,
                of_ref, ob_ref, sfin_ref, st_ref, dm_ref, rd_ref, *, hb):
    c = RET_CHUNK
    s = pl.program_id(2)

    @pl.when(s == 0)
    def _():
        st_ref[...] = s0_ref[...]
        row = lax.broadcasted_iota(jnp.int32, (c, c), 0).astype(F32)
        col = lax.broadcasted_iota(jnp.int32, (c, c), 1).astype(F32)
        for d in range(2):
            for hh in range(hb):
                ld = -jnp.exp(dec_ref[d, hh])
                if d == 0:
                    dist, keep = row - col, row >= col
                    q_pow, k_pow = row + 1.0, (c - 1.0) - row
                else:
                    dist, keep = col - row, col > row
                    q_pow, k_pow = c - row, row
                dm_ref[d, hh] = jnp.where(keep, jnp.exp(ld * jnp.maximum(dist, 0.0)), 0.0)
                rd_ref[d, hh, 0] = jnp.exp(ld * q_pow)
                rd_ref[d, hh, 1] = jnp.exp(ld * k_pow)

    streams = ((qf_ref, kf_ref, vf_ref, of_ref), (qb_ref, kb_ref, vb_ref, ob_ref))
    for d, (q_ref, k_ref, v_ref, o_ref) in enumerate(streams):
        for hh in range(hb):
            cols = slice(hh * RET_DIM, (hh + 1) * RET_DIM)
            q = q_ref[:, cols]
            k = k_ref[:, cols]
            v = v_ref[:, cols]
            state = st_ref[d, hh]
            att = _dot_nt(q, k) * dm_ref[d, hh]
            o = _dot(att.astype(BF16), v) + rd_ref[d, hh, 0] * _dot(q, state.astype(BF16))
            kt = (k.astype(F32) * rd_ref[d, hh, 1]).astype(BF16)
            chunk_decay = jnp.exp(-jnp.exp(dec_ref[d, hh]) * float(c))
            st_ref[d, hh] = state * chunk_decay + _dot_tn(kt, v)
            o_ref[:, cols] = o

    @pl.when(s == pl.num_programs(2) - 1)
    def _():
        sfin_ref[...] = st_ref[...]


def retention(q, k, v, dec, s0, hb=2):
    b, n, w = q.shape
    h = w // RET_DIM
    c = RET_CHUNK
    nc = n // c
    fw = pl.BlockSpec((None, c, hb * RET_DIM), lambda bb, g, s: (bb, s, g))
    bw = pl.BlockSpec((None, c, hb * RET_DIM), lambda bb, g, s: (bb, nc - 1 - s, g))
    st = pl.BlockSpec((None, 2, hb, RET_DIM, RET_DIM), lambda bb, g, s: (bb, 0, g, 0, 0))
    return pl.pallas_call(
        functools.partial(_ret_kernel, hb=hb),
        grid=(b, h // hb, nc),
        in_specs=[pl.BlockSpec((2, hb, 1, RET_DIM), lambda bb, g, s: (0, g, 0, 0)),
                  fw, fw, fw, bw, bw, bw, st],
        out_specs=[fw, bw, st],
        out_shape=[jax.ShapeDtypeStruct((b, n, w), F32),
                   jax.ShapeDtypeStruct((b, n, w), F32),
                   jax.ShapeDtypeStruct(s0.shape, F32)],
        scratch_shapes=[pltpu.VMEM((2, hb, RET_DIM, RET_DIM), F32),
                        pltpu.VMEM((2, hb, c, c), F32),
                        pltpu.VMEM((2, hb, 2, c, RET_DIM), F32)],
        compiler_params=_cparams(("parallel", "parallel", "arbitrary")),
        name="retention",
    )(dec, q, k, v, q, k, v, s0)


def _ret_out_kernel(of_ref, ob_ref, g_ref, o_ref):
    for h in range(of_ref.shape[1] // RET_DIM):
        cols = slice(h * RET_DIM, (h + 1) * RET_DIM)
        o = of_ref[:, cols] + ob_ref[:, cols]
        ms = jnp.mean(o * o, axis=-1, keepdims=True)
        g = g_ref[:, cols].astype(F32)
        o_ref[:, cols] = (g * _sigmoid(g) * (o * lax.rsqrt(ms + EPS))).astype(o_ref.dtype)


def ret_out(o_f, o_b, p, tm=256):
    m, w = o_f.shape
    spec = pl.BlockSpec((tm, w), lambda i: (i, 0))
    return pl.pallas_call(
        _ret_out_kernel,
        grid=(m // tm,),
        in_specs=[spec, spec, pl.BlockSpec((tm, w), lambda i: (i, 3))],
        out_specs=spec,
        out_shape=jax.ShapeDtypeStruct((m, w), BF16),
        compiler_params=_cparams(("parallel",)),
        name="ret_out",
    )(o_f, o_b, p)


def _rope_tables(rows, head_dim):
    row = jnp.repeat(jnp.arange(rows), GRID_W).astype(F32)
    col = jnp.tile(jnp.arange(GRID_W), rows).astype(F32)
    n_freq = head_dim // 4
    inv_freq = ROPE_THETA ** (-jnp.arange(n_freq, dtype=F32) / n_freq)
    ang = jnp.concatenate([row[:, None] * inv_freq, col[:, None] * inv_freq], axis=-1)
    sign = jnp.tile(jnp.array([-1.0, 1.0], F32), head_dim // 2)
    return (jnp.repeat(jnp.cos(ang), 2, axis=-1),
            jnp.repeat(jnp.sin(ang), 2, axis=-1) * sign)


def _row_tile(m, pref):
    return pref if m % pref == 0 else m


def _ffn(h, mods, norm_gain, wg, wu, wd):
    sh2, sc2, g2 = mods
    m = h.shape[0]
    hn = norm_mod(h, norm_gain, sc2, sh2)
    act = matmul_swiglu(hn, wg, wu, tm=_row_tile(m, 1024), tn=512)
    return matmul_resid([act], wd, h, g2, tm=512, tn=512)


def kernel(x, c, ctx, c_ctx, norm_mix, norm_ffn, mod_down, mod_up, mod_bias,
           ffn_gate, ffn_up, ffn_down, ev_w_in, ev_w_out, sgu_norm, sgu_w, sgu_b,
           q_norm, k_norm, ret_w_in, ret_w_out, ret_decay_fwd, ret_decay_bwd):
    b, n, d = x.shape
    n_ctx = ctx.shape[1]
    depth = norm_mix.shape[0]
    hidden = ffn_gate.shape[-1]
    hidden_pad = -(-hidden // 512) * 512
    a_width = sgu_w.shape[1] * SGU_GROUP
    n_q = (ev_w_out.shape[1] - a_width) // HEAD_DIM
    ret_w = ret_w_out.shape[1]
    ret_heads = ret_w // RET_DIM

    cv = jnp.concatenate([c, c_ctx[None], jnp.zeros((8 - b - 1, d), F32)], axis=0)
    mods = adaln_all(cv, mod_down, mod_up, mod_bias)

    cos_a, sin_a = _rope_tables(n // GRID_W, HEAD_DIM)
    cos_r, sin_r = _rope_tables(n // GRID_W, RET_DIM)

    h_lat = x.reshape(b * n, d)
    h_ctx = ctx.reshape(b * n_ctx, d)
    for i in range(depth):
        last = i == depth - 1
        j = i // 2
        lat_mod = [mods[i, :b, t * d:(t + 1) * d].reshape(b, 1, d) for t in range(N_MOD)]
        ctx_mod = [mods[i, b:b + 1, t * d:(t + 1) * d].reshape(1, 1, d) for t in range(N_MOD)]
        xn = norm_mod(h_lat, norm_mix[i], lat_mod[1], lat_mod[0])
        cn = norm_mod(h_ctx, norm_mix[i], ctx_mod[1], ctx_mod[0])
        if i % 2 == 0:
            w_in = ev_w_in[j].astype(BF16)
            w_out = ev_w_out[j].astype(BF16)
            p_l = matmul(xn, w_in, F32, tm=1024, tn=512)
            p_c = matmul(cn, w_in, F32, tm=b * n_ctx, tn=512)
            ql, kl, vl = qkv_prep(p_l, q_norm[j], k_norm[j], cos_a, sin_a,
                                  2 * a_width, n_q, KV_HEADS, rope=True)
            qc, kc, vc = qkv_prep(p_c, q_norm[j], k_norm[j], cos_a, sin_a,
                                  2 * a_width, n_q, KV_HEADS, rope=False)
            kw = KV_HEADS * HEAD_DIM
            k_all = jnp.concatenate([kc.reshape(b, n_ctx, kw), kl.reshape(b, n, kw)], axis=1)
            v_all = jnp.concatenate([vc.reshape(b, n_ctx, kw), vl.reshape(b, n, kw)], axis=1)
            tk = next(t for t in (768, 512, 256) if (n_ctx + n) % t == 0)
            att_l = flash_attention(ql.reshape(b, n, -1), k_all, v_all, tq=256, tk=tk)
            sgu_l = sgu(p_l, sgu_norm[j], sgu_w[j], sgu_b[j])
            h_lat = matmul_resid([sgu_l, att_l.reshape(b * n, -1)], w_out, h_lat, lat_mod[2],
                                 tm=1024, tn=512)
            if not last:
                att_c = flash_attention(qc.reshape(b, n_ctx, -1), kc.reshape(b, n_ctx, kw),
                                        vc.reshape(b, n_ctx, kw), tq=n_ctx, tk=n_ctx)
                sgu_c = sgu(p_c, sgu_norm[j], sgu_w[j], sgu_b[j])
                h_ctx = matmul_resid([sgu_c, att_c.reshape(b * n_ctx, -1)], w_out, h_ctx,
                                     ctx_mod[2], tm=b * n_ctx, tn=512)
        else:
            w_in = ret_w_in[j].astype(BF16)
            w_out = ret_w_out[j].astype(BF16)
            p_l = matmul(xn, w_in, F32, tm=1024, tn=512)
            p_c = matmul(cn, w_in, F32, tm=b * n_ctx, tn=512)
            ql, kl, vl = ret_prep(p_l, cos_r, sin_r, ret_w, rope=True)
            qc, kc, vc = ret_prep(p_c, cos_r, sin_r, ret_w, rope=False)
            dec = jnp.stack([ret_decay_fwd[j], ret_decay_bwd[j]]).astype(F32)
            dec = jnp.broadcast_to(dec[:, :, None, None], (2, ret_heads, 1, RET_DIM))
            s0 = jnp.zeros((b, 2, ret_heads, RET_DIM, RET_DIM), F32)
            shp_c, shp_l = (b, n_ctx, ret_w), (b, n, ret_w)
            ocf, ocb, s_ctx = retention(qc.reshape(shp_c), kc.reshape(shp_c),
                                        vc.reshape(shp_c), dec, s0)
            olf, olb, _ = retention(ql.reshape(shp_l), kl.reshape(shp_l),
                                    vl.reshape(shp_l), dec, s_ctx)
            y_l = ret_out(olf.reshape(b * n, ret_w), olb.reshape(b * n, ret_w), p_l)
            h_lat = matmul_resid([y_l], w_out, h_lat, lat_mod[2], tm=1024, tn=512)
            if not last:
                y_c = ret_out(ocf.reshape(b * n_ctx, ret_w), ocb.reshape(b * n_ctx, ret_w), p_c)
                h_ctx = matmul_resid([y_c], w_out, h_ctx, ctx_mod[2], tm=b * n_ctx, tn=512)

        pad = ((0, 0), (0, hidden_pad - hidden))
        wg = jnp.pad(ffn_gate[i], pad).astype(BF16)
        wu = jnp.pad(ffn_up[i], pad).astype(BF16)
        wd = jnp.pad(ffn_down[i], (pad[1], pad[0])).astype(BF16)
        h_lat = _ffn(h_lat, lat_mod[3:], norm_ffn[i], wg, wu, wd)
        if not last:
            h_ctx = _ffn(h_ctx, ctx_mod[3:], norm_ffn[i], wg, wu, wd)
    return h_lat.reshape(b, n, d)
```

```python
import functools

import jax
import jax.numpy as jnp
from jax import lax
from jax.experimental import pallas as pl
from jax.experimental.pallas import tpu as pltpu

F32 = jnp.float32
BF16 = jnp.bfloat16

EPS = 1e-6
GRID_W = 64
N_MOD = 6
ROPE_THETA = 10000.0
SGU_CHUNK = 128
SGU_GROUP = 128
HEAD_DIM = 128
KV_HEADS = 4
RET_DIM = 256
RET_CHUNK = 256
LANES = 128
LOG2E = 1.4426950408889634
FLASH_ROWS = 64
VMEM_LIMIT = 56 * 1024 * 1024


def _cparams(sem):
    return pltpu.CompilerParams(dimension_semantics=sem, vmem_limit_bytes=VMEM_LIMIT)


def _sigmoid(x):
    return 1.0 / (1.0 + jnp.exp(-x))


def _gelu(x):
    return 0.5 * x * (1.0 + jnp.tanh(0.7978845608028654 * (x + 0.044715 * (x * x * x))))


def _dot(a, b):
    return jnp.dot(a, b, preferred_element_type=F32)


def _dot_nt(a, b):
    return lax.dot_general(a, b, (((1,), (1,)), ((), ())), preferred_element_type=F32)


def _dot_tn(a, b):
    return lax.dot_general(a, b, (((0,), (0,)), ((), ())), preferred_element_type=F32)


def _adaln_kernel(cv_ref, down_ref, up_ref, bias_ref, o_ref, hd_ref):
    @pl.when(pl.program_id(1) == 0)
    def _():
        cv = cv_ref[...]
        hd_ref[...] = _dot(cv * _sigmoid(cv), down_ref[...])

    o_ref[...] = _dot(hd_ref[...], up_ref[...]) + bias_ref[...]


def adaln_all(cv, down, up, bias, tn=2048):
    depth, d, rank = down.shape
    nout = up.shape[-1]
    rows = cv.shape[0]
    return pl.pallas_call(
        _adaln_kernel,
        grid=(depth, nout // tn),
        in_specs=[
            pl.BlockSpec((rows, d), lambda l, j: (0, 0)),
            pl.BlockSpec((None, d, rank), lambda l, j: (l, 0, 0)),
            pl.BlockSpec((None, rank, tn), lambda l, j: (l, 0, j)),
            pl.BlockSpec((None, 1, tn), lambda l, j: (l, 0, j)),
        ],
        out_specs=pl.BlockSpec((None, rows, tn), lambda l, j: (l, 0, j)),
        out_shape=jax.ShapeDtypeStruct((depth, rows, nout), F32),
        scratch_shapes=[pltpu.VMEM((rows, rank), F32)],
        compiler_params=_cparams(("arbitrary", "arbitrary")),
        name="adaln",
    )(cv, down, up, bias.reshape(depth, 1, nout))


def _norm_mod_kernel(x_ref, gain_ref, scale_ref, shift_ref, o_ref):
    x = x_ref[...]
    ms = jnp.mean(x * x, axis=-1, keepdims=True)
    xn = x * lax.rsqrt(ms + EPS) * gain_ref[...]
    o_ref[...] = (xn * (1.0 + scale_ref[...]) + shift_ref[...]).astype(o_ref.dtype)


def norm_mod(x, gain, scale, shift, tm=512):
    m, d = x.shape
    nb = scale.shape[0]
    assert (m // tm) % nb == 0
    tpb = (m // tm) // nb
    vec = pl.BlockSpec((None, 1, d), lambda i: (i // tpb, 0, 0))
    return pl.pallas_call(
        _norm_mod_kernel,
        grid=(m // tm,),
        in_specs=[pl.BlockSpec((tm, d), lambda i: (i, 0)),
                  pl.BlockSpec((1, d), lambda i: (0, 0)), vec, vec],
        out_specs=pl.BlockSpec((tm, d), lambda i: (i, 0)),
        out_shape=jax.ShapeDtypeStruct((m, d), BF16),
        compiler_params=_cparams(("parallel",)),
        name="norm_mod",
    )(x, gain.reshape(1, d), scale, shift)


def _mm_store_kernel(a_ref, w_ref, o_ref):
    o_ref[...] = _dot(a_ref[...], w_ref[...]).astype(o_ref.dtype)


def matmul(a, w, layer, out_dtype, tm, tn):
    m, k = a.shape
    n = w.shape[2]
    return pl.pallas_call(
        _mm_store_kernel,
        grid=(m // tm, n // tn),
        in_specs=[pl.BlockSpec((tm, k), lambda i, j: (i, 0)),
                  pl.BlockSpec((None, k, tn), lambda i, j: (layer, 0, j))],
        out_specs=pl.BlockSpec((tm, tn), lambda i, j: (i, j)),
        out_shape=jax.ShapeDtypeStruct((m, n), out_dtype),
        compiler_params=_cparams(("parallel", "arbitrary")),
        name="matmul",
    )(a, w)


def _mm_resid_kernel(*refs, n_a):
    a_refs, w_refs = refs[:n_a], refs[n_a:2 * n_a]
    resid_ref, gate_ref, o_ref = refs[2 * n_a:]
    acc = _dot(a_refs[0][...], w_refs[0][...])
    for a_ref, w_ref in zip(a_refs[1:], w_refs[1:]):
        acc = acc + _dot(a_ref[...], w_ref[...])
    o_ref[...] = resid_ref[...] + gate_ref[...] * acc


def matmul_resid(a_list, w, layer, resid, gate, tm, tn):
    m, n = resid.shape
    nb = gate.shape[0]
    assert (m // tm) % nb == 0
    tpb = (m // tm) // nb
    n_a = len(a_list)
    in_specs, w_args, off = [], [], 0
    for a in a_list:
        in_specs.append(pl.BlockSpec((tm, a.shape[1]), lambda i, j: (i, 0)))
    for a in a_list:
        ka = a.shape[1]
        assert off % ka == 0
        in_specs.append(pl.BlockSpec((None, ka, tn), lambda i, j, r=off // ka: (layer, r, j)))
        w_args.append(w)
        off += ka
    assert off == w.shape[1]
    in_specs += [pl.BlockSpec((tm, tn), lambda i, j: (i, j)),
                 pl.BlockSpec((None, 1, tn), lambda i, j: (i // tpb, 0, j))]
    return pl.pallas_call(
        functools.partial(_mm_resid_kernel, n_a=n_a),
        grid=(m // tm, n // tn),
        in_specs=in_specs,
        out_specs=pl.BlockSpec((tm, tn), lambda i, j: (i, j)),
        out_shape=jax.ShapeDtypeStruct((m, n), F32),
        compiler_params=_cparams(("parallel", "arbitrary")),
        name="matmul_resid",
    )(*a_list, *w_args, resid, gate)


def _mm_swiglu_kernel(a_ref, wg_ref, wu_ref, o_ref):
    a = a_ref[...]
    g = _dot(a, wg_ref[...])
    u = _dot(a, wu_ref[...])
    o_ref[...] = (g * _sigmoid(g) * u).astype(o_ref.dtype)


def matmul_swiglu(a, wg, wu, layer, tm, tn):
    m, k = a.shape
    n = wg.shape[2]
    wspec = pl.BlockSpec((None, k, tn), lambda i, j: (layer, 0, j))
    return pl.pallas_call(
        _mm_swiglu_kernel,
        grid=(m // tm, n // tn),
        in_specs=[pl.BlockSpec((tm, k), lambda i, j: (i, 0)), wspec, wspec],
        out_specs=pl.BlockSpec((tm, tn), lambda i, j: (i, j)),
        out_shape=jax.ShapeDtypeStruct((m, n), BF16),
        compiler_params=_cparams(("parallel", "arbitrary")),
        name="matmul_swiglu",
    )(a, wg, wu)


def _rope_slab(x, cos, sin_signed, even):
    swapped = jnp.where(even, pltpu.roll(x, LANES - 1, 1), pltpu.roll(x, 1, 1))
    return x * cos + swapped * sin_signed


def _even_lanes(rows):
    return (lax.broadcasted_iota(jnp.int32, (rows, LANES), 1) % 2) == 0


def _sgu_kernel(u_ref, v_ref, gain_ref, w_ref, b_ref, o_ref):
    tm = u_ref.shape[0]
    groups = w_ref.shape[0]
    v = _gelu(v_ref[...].astype(F32))
    ms = jnp.mean(v * v, axis=-1, keepdims=True)
    vn = (v * lax.rsqrt(ms + EPS) * gain_ref[...]).astype(BF16)
    bias = b_ref[...]
    for c in range(tm // SGU_CHUNK):
        rows = slice(c * SGU_CHUNK, (c + 1) * SGU_CHUNK)
        for g in range(groups):
            cols = slice(g * SGU_GROUP, (g + 1) * SGU_GROUP)
            mixed = _dot(w_ref[g], vn[rows, cols]) + bias[:, g:g + 1]
            u = _gelu(u_ref[rows, cols].astype(F32))
            o_ref[rows, cols] = (u * mixed).astype(o_ref.dtype)


def sgu(p, gain, w_s, b_s, tm=256):
    m = p.shape[0]
    groups = w_s.shape[0]
    width = groups * SGU_GROUP
    return pl.pallas_call(
        _sgu_kernel,
        grid=(m // tm,),
        in_specs=[pl.BlockSpec((tm, width), lambda i: (i, 0)),
                  pl.BlockSpec((tm, width), lambda i: (i, 1)),
                  pl.BlockSpec((1, width), lambda i: (0, 0)),
                  pl.BlockSpec((groups, SGU_CHUNK, SGU_CHUNK), lambda i: (0, 0, 0)),
                  pl.BlockSpec((SGU_CHUNK, groups), lambda i: (0, 0))],
        out_specs=pl.BlockSpec((tm, width), lambda i: (i, 0)),
        out_shape=jax.ShapeDtypeStruct((m, width), BF16),
        compiler_params=_cparams(("parallel",)),
        name="sgu",
    )(p, p, gain.reshape(1, width), w_s.astype(BF16), b_s.T)


def _qkv_prep_kernel(q_ref, k_ref, v_ref, qg_ref, kg_ref, cos_ref, sin_ref,
                     qo_ref, ko_ref, vo_ref, *, rope, q_scale):
    tm = q_ref.shape[0]
    even = _even_lanes(tm)
    cos = cos_ref[...]
    sin = sin_ref[...]

    def prep(x, gain):
        ms = jnp.mean(x * x, axis=-1, keepdims=True)
        xn = x * lax.rsqrt(ms + EPS) * gain
        if rope:
            xn = _rope_slab(xn, cos, sin, even)
        return xn

    qg = qg_ref[...]
    kg = kg_ref[...]
    for h in range(q_ref.shape[1] // HEAD_DIM):
        cols = slice(h * HEAD_DIM, (h + 1) * HEAD_DIM)
        qo_ref[:, cols] = (prep(q_ref[:, cols].astype(F32), qg) * q_scale).astype(qo_ref.dtype)
    for h in range(k_ref.shape[1] // HEAD_DIM):
        cols = slice(h * HEAD_DIM, (h + 1) * HEAD_DIM)
        ko_ref[:, cols] = prep(k_ref[:, cols].astype(F32), kg).astype(ko_ref.dtype)
    vo_ref[...] = v_ref[...].astype(vo_ref.dtype)


def qkv_prep(p, q_gain, k_gain, cos, sin, col0, n_q, n_kv, rope, tm=256):
    m = p.shape[0]
    qw, kw = n_q * HEAD_DIM, n_kv * HEAD_DIM
    assert col0 % qw == 0 and (col0 + qw) % kw == 0
    pos_tiles = cos.shape[0] // tm
    tab = pl.BlockSpec((tm, HEAD_DIM), lambda i: (i % pos_tiles, 0))
    kern = functools.partial(_qkv_prep_kernel, rope=rope, q_scale=HEAD_DIM ** -0.5 * LOG2E)
    return pl.pallas_call(
        kern,
        grid=(m // tm,),
        in_specs=[pl.BlockSpec((tm, qw), lambda i: (i, col0 // qw)),
                  pl.BlockSpec((tm, kw), lambda i: (i, (col0 + qw) // kw)),
                  pl.BlockSpec((tm, kw), lambda i: (i, (col0 + qw) // kw + 1)),
                  pl.BlockSpec((1, HEAD_DIM), lambda i: (0, 0)),
                  pl.BlockSpec((1, HEAD_DIM), lambda i: (0, 0)),
                  tab, tab],
        out_specs=[pl.BlockSpec((tm, qw), lambda i: (i, 0)),
                   pl.BlockSpec((tm, kw), lambda i: (i, 0)),
                   pl.BlockSpec((tm, kw), lambda i: (i, 0))],
        out_shape=[jax.ShapeDtypeStruct((m, qw), BF16),
                   jax.ShapeDtypeStruct((m, kw), BF16),
                   jax.ShapeDtypeStruct((m, kw), BF16)],
        compiler_params=_cparams(("parallel",)),
        name="qkv_prep",
    )(p, p, p, q_gain.reshape(1, HEAD_DIM), k_gain.reshape(1, HEAD_DIM), cos, sin)


def _flash_kernel(q_ref, k_ref, v_ref, o_ref, sa_ref, sb_ref, pa_ref, pb_ref,
                  m_ref, l_ref, acc_ref, *, tk, group):
    tq = q_ref.shape[0]
    nk = k_ref.shape[0] // tk
    ncb = tk // LANES
    q = jnp.concatenate(
        [q_ref[:, g * HEAD_DIM:(g + 1) * HEAD_DIM] for g in range(group)], axis=0)
    m_ref[...] = jnp.full(m_ref.shape, -jnp.inf, F32)
    l_ref[...] = jnp.zeros(l_ref.shape, F32)
    acc_ref[...] = jnp.zeros(acc_ref.shape, F32)

    def scores(j, s_ref):
        off = pl.multiple_of(j * tk, tk)
        s_ref[...] = _dot_nt(q, k_ref[pl.ds(off, tk), :])

    def softmax_pv(j, s_ref, p_ref):
        off = pl.multiple_of(j * tk, tk)
        for r0 in range(0, group * tq, FLASH_ROWS):
            rows = slice(r0, r0 + FLASH_ROWS)
            blocks = [s_ref[rows, cb * LANES:(cb + 1) * LANES] for cb in range(ncb)]
            mx = blocks[0]
            for blk in blocks[1:]:
                mx = jnp.maximum(mx, blk)
            m_prev = m_ref[rows, :]
            m_new = jnp.maximum(
                m_prev, jnp.broadcast_to(jnp.max(mx, axis=-1, keepdims=True), mx.shape))
            alpha = jnp.exp2(m_prev - m_new)
            psum = None
            for cb, blk in enumerate(blocks):
                pb = jnp.exp2(blk - m_new)
                psum = pb if psum is None else psum + pb
                p_ref[rows, cb * LANES:(cb + 1) * LANES] = pb.astype(BF16)
            l_ref[rows, :] = alpha * l_ref[rows, :] + psum
            acc_ref[rows, :] = alpha * acc_ref[rows, :]
            m_ref[rows, :] = m_new
        acc_ref[...] += _dot(p_ref[...], v_ref[pl.ds(off, tk), :])

    scores(0, sa_ref)

    def body(jj, carry):
        j = 2 * jj
        scores(j + 1, sb_ref)
        softmax_pv(j, sa_ref, pa_ref)
        scores(j + 2, sa_ref)
        softmax_pv(j + 1, sb_ref, pb_ref)
        return carry

    lax.fori_loop(0, (nk - 1) // 2, body, 0)
    if nk % 2 == 1:
        softmax_pv(nk - 1, sa_ref, pa_ref)
    else:
        scores(nk - 1, sb_ref)
        softmax_pv(nk - 2, sa_ref, pa_ref)
        softmax_pv(nk - 1, sb_ref, pb_ref)

    out = acc_ref[...] / jnp.sum(l_ref[...], axis=-1, keepdims=True)
    for g in range(group):
        o_ref[:, g * HEAD_DIM:(g + 1) * HEAD_DIM] = out[g * tq:(g + 1) * tq].astype(o_ref.dtype)


def flash_attention(q, k, v, tq, tk):
    b, n, qw = q.shape
    s = k.shape[1]
    kvh = k.shape[2] // HEAD_DIM
    group = qw // HEAD_DIM // kvh
    gw = group * HEAD_DIM
    assert s % tk == 0 and n % tq == 0
    return pl.pallas_call(
        functools.partial(_flash_kernel, tk=tk, group=group),
        grid=(b, kvh, n // tq),
        in_specs=[pl.BlockSpec((None, tq, gw), lambda bb, h, i: (bb, i, h)),
                  pl.BlockSpec((None, s, HEAD_DIM), lambda bb, h, i: (bb, 0, h)),
                  pl.BlockSpec((None, s, HEAD_DIM), lambda bb, h, i: (bb, 0, h))],
        out_specs=pl.BlockSpec((None, tq, gw), lambda bb, h, i: (bb, i, h)),
        out_shape=jax.ShapeDtypeStruct((b, n, qw), BF16),
        scratch_shapes=[pltpu.VMEM((group * tq, tk), F32),
                        pltpu.VMEM((group * tq, tk), F32),
                        pltpu.VMEM((group * tq, tk), BF16),
                        pltpu.VMEM((group * tq, tk), BF16),
                        pltpu.VMEM((group * tq, LANES), F32),
                        pltpu.VMEM((group * tq, LANES), F32),
                        pltpu.VMEM((group * tq, HEAD_DIM), F32)],
        compiler_params=_cparams(("parallel", "parallel", "arbitrary")),
        name="flash_attention",
    )(q, k, v)


def _ret_prep_kernel(q_ref, k_ref, v_ref, cos_ref, sin_ref, qo_ref, ko_ref, vo_ref,
                     *, rope, k_scale):
    tm = q_ref.shape[0]
    even = _even_lanes(tm)
    slabs_per_head = RET_DIM // LANES
    for sl in range(q_ref.shape[1] // LANES):
        cols = slice(sl * LANES, (sl + 1) * LANES)
        tcols = slice((sl % slabs_per_head) * LANES, (sl % slabs_per_head + 1) * LANES)
        q = q_ref[:, cols].astype(F32)
        k = k_ref[:, cols].astype(F32) * k_scale
        if rope:
            cos = cos_ref[:, tcols]
            sin = sin_ref[:, tcols]
            q = _rope_slab(q, cos, sin, even)
            k = _rope_slab(k, cos, sin, even)
        qo_ref[:, cols] = q.astype(qo_ref.dtype)
        ko_ref[:, cols] = k.astype(ko_ref.dtype)
    vo_ref[...] = v_ref[...].astype(vo_ref.dtype)


def ret_prep(p, cos, sin, width, rope, tm=256):
    m = p.shape[0]
    pos_tiles = cos.shape[0] // tm
    tab = pl.BlockSpec((tm, RET_DIM), lambda i: (i % pos_tiles, 0))
    out = jax.ShapeDtypeStruct((m, width), BF16)
    ospec = pl.BlockSpec((tm, width), lambda i: (i, 0))
    return pl.pallas_call(
        functools.partial(_ret_prep_kernel, rope=rope, k_scale=RET_DIM ** -0.5),
        grid=(m // tm,),
        in_specs=[pl.BlockSpec((tm, width), lambda i: (i, 0)),
                  pl.BlockSpec((tm, width), lambda i: (i, 1)),
                  pl.BlockSpec((tm, width), lambda i: (i, 2)),
                  tab, tab],
        out_specs=[ospec, ospec, ospec],
        out_shape=[out, out, out],
        compiler_params=_cparams(("parallel",)),
        name="ret_prep",
    )(p, p, p, cos, sin)


def _ret_kernel(dec_ref, qf_ref, kf_ref, vf_ref, qb_ref, kb_ref, vb_ref, s0_ref,
                of_ref, ob_ref, sfin_ref, st_ref, dm_ref, rd_ref, *, hb):
    c = RET_CHUNK
    s = pl.program_id(2)

    @pl.when(s == 0)
    def _():
        st_ref[...] = s0_ref[...]
        row = lax.broadcasted_iota(jnp.int32, (c, c), 0).astype(F32)
        col = lax.broadcasted_iota(jnp.int32, (c, c), 1).astype(F32)
        for d in range(2):
            for hh in range(hb):
                ld = -jnp.exp(dec_ref[d, hh])
                if d == 0:
                    dist, keep = row - col, row >= col
                    q_pow, k_pow = row + 1.0, (c - 1.0) - row
                else:
                    dist, keep = col - row, col > row
                    q_pow, k_pow = c - row, row
                dm_ref[d, hh] = jnp.where(keep, jnp.exp(ld * jnp.maximum(dist, 0.0)), 0.0)
                rd_ref[d, hh, 0] = jnp.exp(ld * q_pow)
                rd_ref[d, hh, 1] = jnp.exp(ld * k_pow)

    streams = ((qf_ref, kf_ref, vf_ref, of_ref), (qb_ref, kb_ref, vb_ref, ob_ref))
    for d, (q_ref, k_ref, v_ref, o_ref) in enumerate(streams):
        for hh in range(hb):
            cols = slice(hh * RET_DIM, (hh + 1) * RET_DIM)
            q = q_ref[:, cols]
            k = k_ref[:, cols]
            v = v_ref[:, cols]
            state = st_ref[d, hh]
            att = _dot_nt(q, k) * dm_ref[d, hh]
            o = _dot(att.astype(BF16), v) + rd_ref[d, hh, 0] * _dot(q, state.astype(BF16))
            kt = (k.astype(F32) * rd_ref[d, hh, 1]).astype(BF16)
            chunk_decay = jnp.exp(-jnp.exp(dec_ref[d, hh]) * float(c))
            st_ref[d, hh] = state * chunk_decay + _dot_tn(kt, v)
            o_ref[:, cols] = o

    @pl.when(s == pl.num_programs(2) - 1)
    def _():
        sfin_ref[...] = st_ref[...]


def retention(q, k, v, dec, s0, hb=2):
    b, n, w = q.shape
    h = w // RET_DIM
    c = RET_CHUNK
    nc = n // c
    fw = pl.BlockSpec((None, c, hb * RET_DIM), lambda bb, g, s: (bb, s, g))
    bw = pl.BlockSpec((None, c, hb * RET_DIM), lambda bb, g, s: (bb, nc - 1 - s, g))
    st = pl.BlockSpec((None, 2, hb, RET_DIM, RET_DIM), lambda bb, g, s: (bb, 0, g, 0, 0))
    return pl.pallas_call(
        functools.partial(_ret_kernel, hb=hb),
        grid=(b, h // hb, nc),
        in_specs=[pl.BlockSpec((2, hb, 1, RET_DIM), lambda bb, g, s: (0, g, 0, 0)),
                  fw, fw, fw, bw, bw, bw, st],
        out_specs=[fw, bw, st],
        out_shape=[jax.ShapeDtypeStruct((b, n, w), F32),
                   jax.ShapeDtypeStruct((b, n, w), F32),
                   jax.ShapeDtypeStruct(s0.shape, F32)],
        scratch_shapes=[pltpu.VMEM((2, hb, RET_DIM, RET_DIM), F32),
                        pltpu.VMEM((2, hb, c, c), F32),
                        pltpu.VMEM((2, hb, 2, c, RET_DIM), F32)],
        compiler_params=_cparams(("parallel", "parallel", "arbitrary")),
        name="retention",
    )(dec, q, k, v, q, k, v, s0)


def _ret_out_kernel(of_ref, ob_ref, g_ref, o_ref):
    for h in range(of_ref.shape[1] // RET_DIM):
        cols = slice(h * RET_DIM, (h + 1) * RET_DIM)
        o = of_ref[:, cols] + ob_ref[:, cols]
        ms = jnp.mean(o * o, axis=-1, keepdims=True)
        g = g_ref[:, cols].astype(F32)
        o_ref[:, cols] = (g * _sigmoid(g) * (o * lax.rsqrt(ms + EPS))).astype(o_ref.dtype)


def ret_out(o_f, o_b, p, tm=256):
    m, w = o_f.shape
    spec = pl.BlockSpec((tm, w), lambda i: (i, 0))
    return pl.pallas_call(
        _ret_out_kernel,
        grid=(m // tm,),
        in_specs=[spec, spec, pl.BlockSpec((tm, w), lambda i: (i, 3))],
        out_specs=spec,
        out_shape=jax.ShapeDtypeStruct((m, w), BF16),
        compiler_params=_cparams(("parallel",)),
        name="ret_out",
    )(o_f, o_b, p)


def _rope_tables(rows, head_dim):
    row = jnp.repeat(jnp.arange(rows), GRID_W).astype(F32)
    col = jnp.tile(jnp.arange(GRID_W), rows).astype(F32)
    n_freq = head_dim // 4
    inv_freq = ROPE_THETA ** (-jnp.arange(n_freq, dtype=F32) / n_freq)
    ang = jnp.concatenate([row[:, None] * inv_freq, col[:, None] * inv_freq], axis=-1)
    sign = jnp.tile(jnp.array([-1.0, 1.0], F32), head_dim // 2)
    return (jnp.repeat(jnp.cos(ang), 2, axis=-1),
            jnp.repeat(jnp.sin(ang), 2, axis=-1) * sign)


def _row_tile(m, pref):
    return pref if m % pref == 0 else m


def _ffn(h, mods, norm_gain, wg, wu, wd, layer):
    sh2, sc2, g2 = mods
    m = h.shape[0]
    hn = norm_mod(h, norm_gain, sc2, sh2)
    act = matmul_swiglu(hn, wg, wu, layer, tm=_row_tile(m, 1024), tn=512)
    return matmul_resid([act], wd, layer, h, g2, tm=512, tn=512)


def kernel(x, c, ctx, c_ctx, norm_mix, norm_ffn, mod_down, mod_up, mod_bias,
           ffn_gate, ffn_up, ffn_down, ev_w_in, ev_w_out, sgu_norm, sgu_w, sgu_b,
           q_norm, k_norm, ret_w_in, ret_w_out, ret_decay_fwd, ret_decay_bwd):
    b, n, d = x.shape
    n_ctx = ctx.shape[1]
    depth = norm_mix.shape[0]
    hidden = ffn_gate.shape[-1]
    hidden_pad = -(-hidden // 512) * 512
    a_width = sgu_w.shape[1] * SGU_GROUP
    n_q = (ev_w_out.shape[1] - a_width) // HEAD_DIM
    ret_w = ret_w_out.shape[1]
    ret_heads = ret_w // RET_DIM

    cv = jnp.concatenate([c, c_ctx[None], jnp.zeros((8 - b - 1, d), F32)], axis=0)
    mods = adaln_all(cv, mod_down, mod_up, mod_bias)

    cos_a, sin_a = _rope_tables(n // GRID_W, HEAD_DIM)
    cos_r, sin_r = _rope_tables(n // GRID_W, RET_DIM)

    pad = ((0, 0), (0, 0), (0, hidden_pad - hidden))
    wg = jnp.pad(ffn_gate.astype(BF16), pad)
    wu = jnp.pad(ffn_up.astype(BF16), pad)
    wd = jnp.pad(ffn_down.astype(BF16), (pad[0], pad[2], pad[1]))
    ev_in, ev_out = ev_w_in.astype(BF16), ev_w_out.astype(BF16)
    ret_in, ret_out_w = ret_w_in.astype(BF16), ret_w_out.astype(BF16)

    h_lat = x.reshape(b * n, d)
    h_ctx = ctx.reshape(b * n_ctx, d)
    for i in range(depth):
        last = i == depth - 1
        j = i // 2
        lat_mod = [mods[i, :b, t * d:(t + 1) * d].reshape(b, 1, d) for t in range(N_MOD)]
        ctx_mod = [mods[i, b:b + 1, t * d:(t + 1) * d].reshape(1, 1, d) for t in range(N_MOD)]
        xn = norm_mod(h_lat, norm_mix[i], lat_mod[1], lat_mod[0])
        cn = norm_mod(h_ctx, norm_mix[i], ctx_mod[1], ctx_mod[0])
        if i % 2 == 0:
            p_l = matmul(xn, ev_in, j, BF16, tm=1024, tn=512)
            p_c = matmul(cn, ev_in, j, BF16, tm=b * n_ctx, tn=512)
            ql, kl, vl = qkv_prep(p_l, q_norm[j], k_norm[j], cos_a, sin_a,
                                  2 * a_width, n_q, KV_HEADS, rope=True)
            qc, kc, vc = qkv_prep(p_c, q_norm[j], k_norm[j], cos_a, sin_a,
                                  2 * a_width, n_q, KV_HEADS, rope=False)
            kw = KV_HEADS * HEAD_DIM
            k_all = jnp.concatenate([kc.reshape(b, n_ctx, kw), kl.reshape(b, n, kw)], axis=1)
            v_all = jnp.concatenate([vc.reshape(b, n_ctx, kw), vl.reshape(b, n, kw)], axis=1)
            tk = next(t for t in (768, 512, 256) if (n_ctx + n) % t == 0)
            att_l = flash_attention(ql.reshape(b, n, -1), k_all, v_all, tq=256, tk=tk)
            sgu_l = sgu(p_l, sgu_norm[j], sgu_w[j], sgu_b[j])
            h_lat = matmul_resid([sgu_l, att_l.reshape(b * n, -1)], ev_out, j, h_lat,
                                 lat_mod[2], tm=1024, tn=512)
            if not last:
                att_c = flash_attention(qc.reshape(b, n_ctx, -1), kc.reshape(b, n_ctx, kw),
                                        vc.reshape(b, n_ctx, kw), tq=n_ctx, tk=n_ctx)
                sgu_c = sgu(p_c, sgu_norm[j], sgu_w[j], sgu_b[j])
                h_ctx = matmul_resid([sgu_c, att_c.reshape(b * n_ctx, -1)], ev_out, j, h_ctx,
                                     ctx_mod[2], tm=b * n_ctx, tn=512)
        else:
            p_l = matmul(xn, ret_in, j, BF16, tm=1024, tn=512)
            p_c = matmul(cn, ret_in, j, BF16, tm=b * n_ctx, tn=512)
            ql, kl, vl = ret_prep(p_l, cos_r, sin_r, ret_w, rope=True)
            qc, kc, vc = ret_prep(p_c, cos_r, sin_r, ret_w, rope=False)
            dec = jnp.stack([ret_decay_fwd[j], ret_decay_bwd[j]]).astype(F32)
            dec = jnp.broadcast_to(dec[:, :, None, None], (2, ret_heads, 1, RET_DIM))
            s0 = jnp.zeros((b, 2, ret_heads, RET_DIM, RET_DIM), F32)
            shp_c, shp_l = (b, n_ctx, ret_w), (b, n, ret_w)
            ocf, ocb, s_ctx = retention(qc.reshape(shp_c), kc.reshape(shp_c),
                                        vc.reshape(shp_c), dec, s0)
            olf, olb, _ = retention(ql.reshape(shp_l), kl.reshape(shp_l),
                                    vl.reshape(shp_l), dec, s_ctx)
            y_l = ret_out(olf.reshape(b * n, ret_w), olb.reshape(b * n, ret_w), p_l)
            h_lat = matmul_resid([y_l], ret_out_w, j, h_lat, lat_mod[2], tm=1024, tn=512)
            if not last:
                y_c = ret_out(ocf.reshape(b * n_ctx, ret_w), ocb.reshape(b * n_ctx, ret_w), p_c)
                h_ctx = matmul_resid([y_c], ret_out_w, j, h_ctx, ctx_mod[2],
                                     tm=b * n_ctx, tn=512)

        h_lat = _ffn(h_lat, lat_mod[3:], norm_ffn[i], wg, wu, wd, i)
        if not last:
            h_ctx = _ffn(h_ctx, ctx_mod[3:], norm_ffn[i], wg, wu, wd, i)
    return h_lat.reshape(b, n, d)
```

```python
import functools

import jax
import jax.numpy as jnp
from jax import lax
from jax.experimental import pallas as pl
from jax.experimental.pallas import tpu as pltpu

F32 = jnp.float32
BF16 = jnp.bfloat16

EPS = 1e-6
GRID_W = 64
N_MOD = 6
ROPE_THETA = 10000.0
SGU_CHUNK = 128
SGU_GROUP = 128
HEAD_DIM = 128
KV_HEADS = 4
RET_DIM = 256
RET_CHUNK = 256
LANES = 128
LOG2E = 1.4426950408889634
FLASH_ROWS = 64
VMEM_LIMIT = 56 * 1024 * 1024


def _cparams(sem):
    return pltpu.CompilerParams(dimension_semantics=sem, vmem_limit_bytes=VMEM_LIMIT)


def _sigmoid(x):
    return 1.0 / (1.0 + jnp.exp(-x))


def _gelu(x):
    return 0.5 * x * (1.0 + jnp.tanh(0.7978845608028654 * (x + 0.044715 * (x * x * x))))


def _dot(a, b):
    return jnp.dot(a, b, preferred_element_type=F32)


def _dot_nt(a, b):
    return lax.dot_general(a, b, (((1,), (1,)), ((), ())), preferred_element_type=F32)


def _dot_tn(a, b):
    return lax.dot_general(a, b, (((0,), (0,)), ((), ())), preferred_element_type=F32)


def _adaln_kernel(cv_ref, down_ref, up_ref, bias_ref, o_ref, hd_ref):
    @pl.when(pl.program_id(1) == 0)
    def _():
        cv = cv_ref[...]
        hd_ref[...] = _dot(cv * _sigmoid(cv), down_ref[...])

    o_ref[...] = _dot(hd_ref[...], up_ref[...]) + bias_ref[...]


def adaln_all(cv, down, up, bias, tn=2048):
    depth, d, rank = down.shape
    nout = up.shape[-1]
    rows = cv.shape[0]
    return pl.pallas_call(
        _adaln_kernel,
        grid=(depth, nout // tn),
        in_specs=[
            pl.BlockSpec((rows, d), lambda l, j: (0, 0)),
            pl.BlockSpec((None, d, rank), lambda l, j: (l, 0, 0)),
            pl.BlockSpec((None, rank, tn), lambda l, j: (l, 0, j)),
            pl.BlockSpec((None, 1, tn), lambda l, j: (l, 0, j)),
        ],
        out_specs=pl.BlockSpec((None, rows, tn), lambda l, j: (l, 0, j)),
        out_shape=jax.ShapeDtypeStruct((depth, rows, nout), F32),
        scratch_shapes=[pltpu.VMEM((rows, rank), F32)],
        compiler_params=_cparams(("arbitrary", "arbitrary")),
        name="adaln",
    )(cv, down, up, bias.reshape(depth, 1, nout))


def _norm_mod_kernel(x_ref, gain_ref, scale_ref, shift_ref, o_ref):
    x = x_ref[...]
    ms = jnp.mean(x * x, axis=-1, keepdims=True)
    xn = x * lax.rsqrt(ms + EPS) * gain_ref[...]
    o_ref[...] = (xn * (1.0 + scale_ref[...]) + shift_ref[...]).astype(o_ref.dtype)


def norm_mod(x, gain, scale, shift, tm=512):
    m, d = x.shape
    nb = scale.shape[0]
    assert (m // tm) % nb == 0
    tpb = (m // tm) // nb
    vec = pl.BlockSpec((None, 1, d), lambda i: (i // tpb, 0, 0))
    return pl.pallas_call(
        _norm_mod_kernel,
        grid=(m // tm,),
        in_specs=[pl.BlockSpec((tm, d), lambda i: (i, 0)),
                  pl.BlockSpec((1, d), lambda i: (0, 0)), vec, vec],
        out_specs=pl.BlockSpec((tm, d), lambda i: (i, 0)),
        out_shape=jax.ShapeDtypeStruct((m, d), BF16),
        compiler_params=_cparams(("parallel",)),
        name="norm_mod",
    )(x, gain.reshape(1, d), scale, shift)


def _mm_store_kernel(a_ref, w_ref, o_ref):
    o_ref[...] = _dot(a_ref[...], w_ref[...]).astype(o_ref.dtype)


def matmul(a, w, layer, out_dtype, tm, tn):
    m, k = a.shape
    n = w.shape[2]
    return pl.pallas_call(
        _mm_store_kernel,
        grid=(m // tm, n // tn),
        in_specs=[pl.BlockSpec((tm, k), lambda i, j: (i, 0)),
                  pl.BlockSpec((None, k, tn), lambda i, j: (layer, 0, j))],
        out_specs=pl.BlockSpec((tm, tn), lambda i, j: (i, j)),
        out_shape=jax.ShapeDtypeStruct((m, n), out_dtype),
        compiler_params=_cparams(("parallel", "arbitrary")),
        name="matmul",
    )(a, w)


def _mm_resid_kernel(*refs, n_a):
    a_refs, w_refs = refs[:n_a], refs[n_a:2 * n_a]
    resid_ref, gate_ref, o_ref = refs[2 * n_a:]
    acc = _dot(a_refs[0][...], w_refs[0][...])
    for a_ref, w_ref in zip(a_refs[1:], w_refs[1:]):
        acc = acc + _dot(a_ref[...], w_ref[...])
    o_ref[...] = resid_ref[...] + gate_ref[...] * acc


def matmul_resid(a_list, w, layer, resid, gate, tm, tn):
    m, n = resid.shape
    nb = gate.shape[0]
    assert (m // tm) % nb == 0
    tpb = (m // tm) // nb
    n_a = len(a_list)
    in_specs, w_args, off = [], [], 0
    for a in a_list:
        in_specs.append(pl.BlockSpec((tm, a.shape[1]), lambda i, j: (i, 0)))
    for a in a_list:
        ka = a.shape[1]
        assert off % ka == 0
        in_specs.append(pl.BlockSpec((None, ka, tn), lambda i, j, r=off // ka: (layer, r, j)))
        w_args.append(w)
        off += ka
    assert off == w.shape[1]
    in_specs += [pl.BlockSpec((tm, tn), lambda i, j: (i, j)),
                 pl.BlockSpec((None, 1, tn), lambda i, j: (i // tpb, 0, j))]
    return pl.pallas_call(
        functools.partial(_mm_resid_kernel, n_a=n_a),
        grid=(m // tm, n // tn),
        in_specs=in_specs,
        out_specs=pl.BlockSpec((tm, tn), lambda i, j: (i, j)),
        out_shape=jax.ShapeDtypeStruct((m, n), F32),
        compiler_params=_cparams(("parallel", "arbitrary")),
        name="matmul_resid",
    )(*a_list, *w_args, resid, gate)


def _mm_swiglu_kernel(a_ref, wg_ref, wu_ref, o_ref):
    a = a_ref[...]
    g = _dot(a, wg_ref[...])
    u = _dot(a, wu_ref[...])
    o_ref[...] = (g * _sigmoid(g) * u).astype(o_ref.dtype)


def matmul_swiglu(a, wg, wu, layer, tm, tn):
    m, k = a.shape
    n = wg.shape[2]
    wspec = pl.BlockSpec((None, k, tn), lambda i, j: (layer, 0, j))
    return pl.pallas_call(
        _mm_swiglu_kernel,
        grid=(m // tm, n // tn),
        in_specs=[pl.BlockSpec((tm, k), lambda i, j: (i, 0)), wspec, wspec],
        out_specs=pl.BlockSpec((tm, tn), lambda i, j: (i, j)),
        out_shape=jax.ShapeDtypeStruct((m, n), BF16),
        compiler_params=_cparams(("parallel", "arbitrary")),
        name="matmul_swiglu",
    )(a, wg, wu)


def _rope_slab(x, cos, sin_signed, even):
    swapped = jnp.where(even, pltpu.roll(x, LANES - 1, 1), pltpu.roll(x, 1, 1))
    return x * cos + swapped * sin_signed


def _even_lanes(rows):
    return (lax.broadcasted_iota(jnp.int32, (rows, LANES), 1) % 2) == 0


def _sgu_kernel(u_ref, v_ref, gain_ref, w_ref, b_ref, o_ref):
    tm = u_ref.shape[0]
    groups = w_ref.shape[0]
    v = _gelu(v_ref[...].astype(F32))
    ms = jnp.mean(v * v, axis=-1, keepdims=True)
    vn = (v * lax.rsqrt(ms + EPS) * gain_ref[...]).astype(BF16)
    bias = b_ref[...]
    for c in range(tm // SGU_CHUNK):
        rows = slice(c * SGU_CHUNK, (c + 1) * SGU_CHUNK)
        for g in range(groups):
            cols = slice(g * SGU_GROUP, (g + 1) * SGU_GROUP)
            mixed = _dot(w_ref[g], vn[rows, cols]) + bias[:, g:g + 1]
            u = _gelu(u_ref[rows, cols].astype(F32))
            o_ref[rows, cols] = (u * mixed).astype(o_ref.dtype)


def sgu(p, gain, w_s, b_s, tm=256):
    m = p.shape[0]
    groups = w_s.shape[0]
    width = groups * SGU_GROUP
    return pl.pallas_call(
        _sgu_kernel,
        grid=(m // tm,),
        in_specs=[pl.BlockSpec((tm, width), lambda i: (i, 0)),
                  pl.BlockSpec((tm, width), lambda i: (i, 1)),
                  pl.BlockSpec((1, width), lambda i: (0, 0)),
                  pl.BlockSpec((groups, SGU_CHUNK, SGU_CHUNK), lambda i: (0, 0, 0)),
                  pl.BlockSpec((SGU_CHUNK, groups), lambda i: (0, 0))],
        out_specs=pl.BlockSpec((tm, width), lambda i: (i, 0)),
        out_shape=jax.ShapeDtypeStruct((m, width), BF16),
        compiler_params=_cparams(("parallel",)),
        name="sgu",
    )(p, p, gain.reshape(1, width), w_s.astype(BF16), b_s.T)


def _qkv_prep_kernel(q_ref, k_ref, v_ref, qg_ref, kg_ref, cos_ref, sin_ref,
                     qo_ref, ko_ref, vo_ref, *, rope, q_scale):
    tm = q_ref.shape[0]
    even = _even_lanes(tm)
    cos = cos_ref[...]
    sin = sin_ref[...]

    def prep(x, gain):
        ms = jnp.mean(x * x, axis=-1, keepdims=True)
        xn = x * lax.rsqrt(ms + EPS) * gain
        if rope:
            xn = _rope_slab(xn, cos, sin, even)
        return xn

    qg = qg_ref[...]
    kg = kg_ref[...]
    for h in range(q_ref.shape[1] // HEAD_DIM):
        cols = slice(h * HEAD_DIM, (h + 1) * HEAD_DIM)
        qo_ref[:, cols] = (prep(q_ref[:, cols].astype(F32), qg) * q_scale).astype(qo_ref.dtype)
    for h in range(k_ref.shape[1] // HEAD_DIM):
        cols = slice(h * HEAD_DIM, (h + 1) * HEAD_DIM)
        ko_ref[:, cols] = prep(k_ref[:, cols].astype(F32), kg).astype(ko_ref.dtype)
    vo_ref[...] = v_ref[...].astype(vo_ref.dtype)


def qkv_prep(p, q_gain, k_gain, cos, sin, col0, n_q, n_kv, rope, tm=256):
    m = p.shape[0]
    qw, kw = n_q * HEAD_DIM, n_kv * HEAD_DIM
    assert col0 % qw == 0 and (col0 + qw) % kw == 0
    pos_tiles = cos.shape[0] // tm
    tab = pl.BlockSpec((tm, HEAD_DIM), lambda i: (i % pos_tiles, 0))
    kern = functools.partial(_qkv_prep_kernel, rope=rope, q_scale=HEAD_DIM ** -0.5 * LOG2E)
    return pl.pallas_call(
        kern,
        grid=(m // tm,),
        in_specs=[pl.BlockSpec((tm, qw), lambda i: (i, col0 // qw)),
                  pl.BlockSpec((tm, kw), lambda i: (i, (col0 + qw) // kw)),
                  pl.BlockSpec((tm, kw), lambda i: (i, (col0 + qw) // kw + 1)),
                  pl.BlockSpec((1, HEAD_DIM), lambda i: (0, 0)),
                  pl.BlockSpec((1, HEAD_DIM), lambda i: (0, 0)),
                  tab, tab],
        out_specs=[pl.BlockSpec((tm, qw), lambda i: (i, 0)),
                   pl.BlockSpec((tm, kw), lambda i: (i, 0)),
                   pl.BlockSpec((tm, kw), lambda i: (i, 0))],
        out_shape=[jax.ShapeDtypeStruct((m, qw), BF16),
                   jax.ShapeDtypeStruct((m, kw), BF16),
                   jax.ShapeDtypeStruct((m, kw), BF16)],
        compiler_params=_cparams(("parallel",)),
        name="qkv_prep",
    )(p, p, p, q_gain.reshape(1, HEAD_DIM), k_gain.reshape(1, HEAD_DIM), cos, sin)


def _flash_kernel(q_ref, k_ref, v_ref, o_ref, sa_ref, sb_ref, pa_ref, pb_ref,
                  m_ref, l_ref, acc_ref, *, tk, group):
    tq = q_ref.shape[0]
    nk = k_ref.shape[0] // tk
    ncb = tk // LANES
    q = jnp.concatenate(
        [q_ref[:, g * HEAD_DIM:(g + 1) * HEAD_DIM] for g in range(group)], axis=0)
    m_ref[...] = jnp.full(m_ref.shape, -jnp.inf, F32)
    l_ref[...] = jnp.zeros(l_ref.shape, F32)
    acc_ref[...] = jnp.zeros(acc_ref.shape, F32)

    def scores(j, s_ref):
        off = pl.multiple_of(j * tk, tk)
        s_ref[...] = _dot_nt(q, k_ref[pl.ds(off, tk), :])

    def softmax_pv(j, s_ref, p_ref):
        off = pl.multiple_of(j * tk, tk)
        for r0 in range(0, group * tq, FLASH_ROWS):
            rows = slice(r0, r0 + FLASH_ROWS)
            blocks = [s_ref[rows, cb * LANES:(cb + 1) * LANES] for cb in range(ncb)]
            mx = blocks[0]
            for blk in blocks[1:]:
                mx = jnp.maximum(mx, blk)
            m_prev = m_ref[rows, :]
            m_new = jnp.maximum(
                m_prev, jnp.broadcast_to(jnp.max(mx, axis=-1, keepdims=True), mx.shape))
            alpha = jnp.exp2(m_prev - m_new)
            psum = None
            for cb, blk in enumerate(blocks):
                pb = jnp.exp2(blk - m_new)
                psum = pb if psum is None else psum + pb
                p_ref[rows, cb * LANES:(cb + 1) * LANES] = pb.astype(BF16)
            l_ref[rows, :] = alpha * l_ref[rows, :] + psum
            acc_ref[rows, :] = alpha * acc_ref[rows, :]
            m_ref[rows, :] = m_new
        acc_ref[...] += _dot(p_ref[...], v_ref[pl.ds(off, tk), :])

    scores(0, sa_ref)

    def body(jj, carry):
        j = 2 * jj
        scores(j + 1, sb_ref)
        softmax_pv(j, sa_ref, pa_ref)
        scores(j + 2, sa_ref)
        softmax_pv(j + 1, sb_ref, pb_ref)
        return carry

    lax.fori_loop(0, (nk - 1) // 2, body, 0)
    if nk % 2 == 1:
        softmax_pv(nk - 1, sa_ref, pa_ref)
    else:
        scores(nk - 1, sb_ref)
        softmax_pv(nk - 2, sa_ref, pa_ref)
        softmax_pv(nk - 1, sb_ref, pb_ref)

    out = acc_ref[...] / jnp.sum(l_ref[...], axis=-1, keepdims=True)
    for g in range(group):
        o_ref[:, g * HEAD_DIM:(g + 1) * HEAD_DIM] = out[g * tq:(g + 1) * tq].astype(o_ref.dtype)


def flash_attention(q, k, v, tq, tk):
    b, n, qw = q.shape
    s = k.shape[1]
    kvh = k.shape[2] // HEAD_DIM
    group = qw // HEAD_DIM // kvh
    gw = group * HEAD_DIM
    assert s % tk == 0 and n % tq == 0
    return pl.pallas_call(
        functools.partial(_flash_kernel, tk=tk, group=group),
        grid=(b, kvh, n // tq),
        in_specs=[pl.BlockSpec((None, tq, gw), lambda bb, h, i: (bb, i, h)),
                  pl.BlockSpec((None, s, HEAD_DIM), lambda bb, h, i: (bb, 0, h)),
                  pl.BlockSpec((None, s, HEAD_DIM), lambda bb, h, i: (bb, 0, h))],
        out_specs=pl.BlockSpec((None, tq, gw), lambda bb, h, i: (bb, i, h)),
        out_shape=jax.ShapeDtypeStruct((b, n, qw), BF16),
        scratch_shapes=[pltpu.VMEM((group * tq, tk), F32),
                        pltpu.VMEM((group * tq, tk), F32),
                        pltpu.VMEM((group * tq, tk), BF16),
                        pltpu.VMEM((group * tq, tk), BF16),
                        pltpu.VMEM((group * tq, LANES), F32),
                        pltpu.VMEM((group * tq, LANES), F32),
                        pltpu.VMEM((group * tq, HEAD_DIM), F32)],
        compiler_params=_cparams(("parallel", "parallel", "arbitrary")),
        name="flash_attention",
    )(q, k, v)


def _ret_prep_kernel(q_ref, k_ref, cos_ref, sin_ref, qo_ref, ko_ref, *, rope, k_scale):
    tm = q_ref.shape[0]
    even = _even_lanes(tm)
    slabs_per_head = RET_DIM // LANES
    for sl in range(q_ref.shape[1] // LANES):
        cols = slice(sl * LANES, (sl + 1) * LANES)
        tcols = slice((sl % slabs_per_head) * LANES, (sl % slabs_per_head + 1) * LANES)
        q = q_ref[:, cols].astype(F32)
        k = k_ref[:, cols].astype(F32) * k_scale
        if rope:
            cos = cos_ref[:, tcols]
            sin = sin_ref[:, tcols]
            q = _rope_slab(q, cos, sin, even)
            k = _rope_slab(k, cos, sin, even)
        qo_ref[:, cols] = q.astype(qo_ref.dtype)
        ko_ref[:, cols] = k.astype(ko_ref.dtype)


def ret_prep(p, cos, sin, width, rope, tm=256):
    m = p.shape[0]
    pos_tiles = cos.shape[0] // tm
    tab = pl.BlockSpec((tm, RET_DIM), lambda i: (i % pos_tiles, 0))
    out = jax.ShapeDtypeStruct((m, width), BF16)
    ospec = pl.BlockSpec((tm, width), lambda i: (i, 0))
    return pl.pallas_call(
        functools.partial(_ret_prep_kernel, rope=rope, k_scale=RET_DIM ** -0.5),
        grid=(m // tm,),
        in_specs=[pl.BlockSpec((tm, width), lambda i: (i, 0)),
                  pl.BlockSpec((tm, width), lambda i: (i, 1)),
                  tab, tab],
        out_specs=[ospec, ospec],
        out_shape=[out, out],
        compiler_params=_cparams(("parallel",)),
        name="ret_prep",
    )(p, p, cos, sin)


def _ret_kernel(dec_ref, qf_ref, kf_ref, vf_ref, qb_ref, kb_ref, vb_ref, s0_ref,
                of_ref, ob_ref, sfin_ref, st_ref, dm_ref, rd_ref, *, hb):
    c = RET_CHUNK
    s = pl.program_id(2)

    @pl.when(s == 0)
    def _():
        st_ref[...] = s0_ref[...]
        row = lax.broadcasted_iota(jnp.int32, (c, c), 0).astype(F32)
        col = lax.broadcasted_iota(jnp.int32, (c, c), 1).astype(F32)
        for d in range(2):
            for hh in range(hb):
                ld = -jnp.exp(dec_ref[d, hh])
                if d == 0:
                    dist, keep = row - col, row >= col
                    q_pow, k_pow = row + 1.0, (c - 1.0) - row
                else:
                    dist, keep = col - row, col > row
                    q_pow, k_pow = c - row, row
                dm_ref[d, hh] = jnp.where(keep, jnp.exp(ld * jnp.maximum(dist, 0.0)), 0.0)
                rd_ref[d, hh, 0] = jnp.exp(ld * q_pow)
                rd_ref[d, hh, 1] = jnp.exp(ld * k_pow)

    streams = ((qf_ref, kf_ref, vf_ref, of_ref), (qb_ref, kb_ref, vb_ref, ob_ref))
    for d, (q_ref, k_ref, v_ref, o_ref) in enumerate(streams):
        for hh in range(hb):
            cols = slice(hh * RET_DIM, (hh + 1) * RET_DIM)
            q = q_ref[:, cols]
            k = k_ref[:, cols]
            v = v_ref[:, cols]
            state = st_ref[d, hh]
            att = _dot_nt(q, k) * dm_ref[d, hh]
            o = _dot(att.astype(BF16), v) + rd_ref[d, hh, 0] * _dot(q, state.astype(BF16))
            kt = (k.astype(F32) * rd_ref[d, hh, 1]).astype(BF16)
            chunk_decay = jnp.exp(-jnp.exp(dec_ref[d, hh]) * float(c))
            st_ref[d, hh] = state * chunk_decay + _dot_tn(kt, v)
            o_ref[:, cols] = o

    @pl.when(s == pl.num_programs(2) - 1)
    def _():
        sfin_ref[...] = st_ref[...]


def retention(q, k, p, dec, s0, hb=4):
    b, n, w = q.shape
    h = w // RET_DIM
    c = RET_CHUNK
    nc = n // c
    v0 = 2 * (h // hb)
    fw = pl.BlockSpec((None, c, hb * RET_DIM), lambda bb, g, s: (bb, s, g))
    bw = pl.BlockSpec((None, c, hb * RET_DIM), lambda bb, g, s: (bb, nc - 1 - s, g))
    vfw = pl.BlockSpec((None, c, hb * RET_DIM), lambda bb, g, s: (bb, s, v0 + g))
    vbw = pl.BlockSpec((None, c, hb * RET_DIM), lambda bb, g, s: (bb, nc - 1 - s, v0 + g))
    st = pl.BlockSpec((None, 2, hb, RET_DIM, RET_DIM), lambda bb, g, s: (bb, 0, g, 0, 0))
    return pl.pallas_call(
        functools.partial(_ret_kernel, hb=hb),
        grid=(b, h // hb, nc),
        in_specs=[pl.BlockSpec((2, hb, 1, RET_DIM), lambda bb, g, s: (0, g, 0, 0)),
                  fw, fw, vfw, bw, bw, vbw, st],
        out_specs=[fw, bw, st],
        out_shape=[jax.ShapeDtypeStruct((b, n, w), F32),
                   jax.ShapeDtypeStruct((b, n, w), F32),
                   jax.ShapeDtypeStruct(s0.shape, F32)],
        scratch_shapes=[pltpu.VMEM((2, hb, RET_DIM, RET_DIM), F32),
                        pltpu.VMEM((2, hb, c, c), F32),
                        pltpu.VMEM((2, hb, 2, c, RET_DIM), F32)],
        compiler_params=_cparams(("parallel", "parallel", "arbitrary")),
        name="retention",
    )(dec, q, k, p, q, k, p, s0)


def _ret_out_kernel(of_ref, ob_ref, g_ref, o_ref):
    for h in range(of_ref.shape[1] // RET_DIM):
        cols = slice(h * RET_DIM, (h + 1) * RET_DIM)
        o = of_ref[:, cols] + ob_ref[:, cols]
        ms = jnp.mean(o * o, axis=-1, keepdims=True)
        g = g_ref[:, cols].astype(F32)
        o_ref[:, cols] = (g * _sigmoid(g) * (o * lax.rsqrt(ms + EPS))).astype(o_ref.dtype)


def ret_out(o_f, o_b, p, tm=256):
    m, w = o_f.shape
    spec = pl.BlockSpec((tm, w), lambda i: (i, 0))
    return pl.pallas_call(
        _ret_out_kernel,
        grid=(m // tm,),
        in_specs=[spec, spec, pl.BlockSpec((tm, w), lambda i: (i, 3))],
        out_specs=spec,
        out_shape=jax.ShapeDtypeStruct((m, w), BF16),
        compiler_params=_cparams(("parallel",)),
        name="ret_out",
    )(o_f, o_b, p)


def _rope_tables(rows, head_dim):
    row = jnp.repeat(jnp.arange(rows), GRID_W).astype(F32)
    col = jnp.tile(jnp.arange(GRID_W), rows).astype(F32)
    n_freq = head_dim // 4
    inv_freq = ROPE_THETA ** (-jnp.arange(n_freq, dtype=F32) / n_freq)
    ang = jnp.concatenate([row[:, None] * inv_freq, col[:, None] * inv_freq], axis=-1)
    sign = jnp.tile(jnp.array([-1.0, 1.0], F32), head_dim // 2)
    return (jnp.repeat(jnp.cos(ang), 2, axis=-1),
            jnp.repeat(jnp.sin(ang), 2, axis=-1) * sign)


def _row_tile(m, pref):
    return pref if m % pref == 0 else m


def _ffn(h, mods, norm_gain, wg, wu, wd, layer):
    sh2, sc2, g2 = mods
    m = h.shape[0]
    hn = norm_mod(h, norm_gain, sc2, sh2)
    act = matmul_swiglu(hn, wg, wu, layer, tm=_row_tile(m, 2048), tn=256)
    return matmul_resid([act], wd, layer, h, g2, tm=512, tn=512)


def kernel(x, c, ctx, c_ctx, norm_mix, norm_ffn, mod_down, mod_up, mod_bias,
           ffn_gate, ffn_up, ffn_down, ev_w_in, ev_w_out, sgu_norm, sgu_w, sgu_b,
           q_norm, k_norm, ret_w_in, ret_w_out, ret_decay_fwd, ret_decay_bwd):
    b, n, d = x.shape
    n_ctx = ctx.shape[1]
    depth = norm_mix.shape[0]
    a_width = sgu_w.shape[1] * SGU_GROUP
    n_q = (ev_w_out.shape[1] - a_width) // HEAD_DIM
    ret_w = ret_w_out.shape[1]
    ret_heads = ret_w // RET_DIM

    cv = jnp.concatenate([c, c_ctx[None], jnp.zeros((8 - b - 1, d), F32)], axis=0)
    mods = adaln_all(cv, mod_down, mod_up, mod_bias)

    cos_a, sin_a = _rope_tables(n // GRID_W, HEAD_DIM)
    cos_r, sin_r = _rope_tables(n // GRID_W, RET_DIM)

    wg, wu, wd = ffn_gate.astype(BF16), ffn_up.astype(BF16), ffn_down.astype(BF16)
    ev_in, ev_out = ev_w_in.astype(BF16), ev_w_out.astype(BF16)
    ret_in, ret_out_w = ret_w_in.astype(BF16), ret_w_out.astype(BF16)

    h_lat = x.reshape(b * n, d)
    h_ctx = ctx.reshape(b * n_ctx, d)
    for i in range(depth):
        last = i == depth - 1
        j = i // 2
        lat_mod = [mods[i, :b, t * d:(t + 1) * d].reshape(b, 1, d) for t in range(N_MOD)]
        ctx_mod = [mods[i, b:b + 1, t * d:(t + 1) * d].reshape(1, 1, d) for t in range(N_MOD)]
        xn = norm_mod(h_lat, norm_mix[i], lat_mod[1], lat_mod[0])
        cn = norm_mod(h_ctx, norm_mix[i], ctx_mod[1], ctx_mod[0])
        if i % 2 == 0:
            p_l = matmul(xn, ev_in, j, BF16, tm=1024, tn=512)
            p_c = matmul(cn, ev_in, j, BF16, tm=b * n_ctx, tn=512)
            ql, kl, vl = qkv_prep(p_l, q_norm[j], k_norm[j], cos_a, sin_a,
                                  2 * a_width, n_q, KV_HEADS, rope=True)
            qc, kc, vc = qkv_prep(p_c, q_norm[j], k_norm[j], cos_a, sin_a,
                                  2 * a_width, n_q, KV_HEADS, rope=False)
            kw = KV_HEADS * HEAD_DIM
            k_all = jnp.concatenate([kc.reshape(b, n_ctx, kw), kl.reshape(b, n, kw)], axis=1)
            v_all = jnp.concatenate([vc.reshape(b, n_ctx, kw), vl.reshape(b, n, kw)], axis=1)
            tk = next(t for t in (768, 512, 256) if (n_ctx + n) % t == 0)
            att_l = flash_attention(ql.reshape(b, n, -1), k_all, v_all, tq=256, tk=tk)
            sgu_l = sgu(p_l, sgu_norm[j], sgu_w[j], sgu_b[j])
            h_lat = matmul_resid([sgu_l, att_l.reshape(b * n, -1)], ev_out, j, h_lat,
                                 lat_mod[2], tm=1024, tn=512)
            if not last:
                att_c = flash_attention(qc.reshape(b, n_ctx, -1), kc.reshape(b, n_ctx, kw),
                                        vc.reshape(b, n_ctx, kw), tq=n_ctx, tk=n_ctx)
                sgu_c = sgu(p_c, sgu_norm[j], sgu_w[j], sgu_b[j])
                h_ctx = matmul_resid([sgu_c, att_c.reshape(b * n_ctx, -1)], ev_out, j, h_ctx,
                                     ctx_mod[2], tm=b * n_ctx, tn=512)
        else:
            p_l = matmul(xn, ret_in, j, BF16, tm=1024, tn=512)
            p_c = matmul(cn, ret_in, j, BF16, tm=b * n_ctx, tn=512)
            ql, kl = ret_prep(p_l, cos_r, sin_r, ret_w, rope=True)
            qc, kc = ret_prep(p_c, cos_r, sin_r, ret_w, rope=False)
            dec = jnp.stack([ret_decay_fwd[j], ret_decay_bwd[j]]).astype(F32)
            dec = jnp.broadcast_to(dec[:, :, None, None], (2, ret_heads, 1, RET_DIM))
            s0 = jnp.zeros((b, 2, ret_heads, RET_DIM, RET_DIM), F32)
            shp_c, shp_l = (b, n_ctx, ret_w), (b, n, ret_w)
            ocf, ocb, s_ctx = retention(qc.reshape(shp_c), kc.reshape(shp_c),
                                        p_c.reshape(b, n_ctx, -1), dec, s0)
            olf, olb, _ = retention(ql.reshape(shp_l), kl.reshape(shp_l),
                                    p_l.reshape(b, n, -1), dec, s_ctx)
            y_l = ret_out(olf.reshape(b * n, ret_w), olb.reshape(b * n, ret_w), p_l)
            h_lat = matmul_resid([y_l], ret_out_w, j, h_lat, lat_mod[2], tm=1024, tn=512)
            if not last:
                y_c = ret_out(ocf.reshape(b * n_ctx, ret_w), ocb.reshape(b * n_ctx, ret_w), p_c)
                h_ctx = matmul_resid([y_c], ret_out_w, j, h_ctx, ctx_mod[2],
                                     tm=b * n_ctx, tn=512)

        h_lat = _ffn(h_lat, lat_mod[3:], norm_ffn[i], wg, wu, wd, i)
        if not last:
            h_ctx = _ffn(h_ctx, ctx_mod[3:], norm_ffn[i], wg, wu, wd, i)
    return h_lat.reshape(b, n, d)
```

```python
import functools

import jax
import jax.numpy as jnp
from jax import lax
from jax.experimental import pallas as pl
from jax.experimental.pallas import tpu as pltpu

F32 = jnp.float32
BF16 = jnp.bfloat16

EPS = 1e-6
GRID_W = 64
N_MOD = 6
ROPE_THETA = 10000.0
SGU_CHUNK = 128
SGU_GROUP = 128
HEAD_DIM = 128
KV_HEADS = 4
RET_DIM = 256
RET_CHUNK = 256
LANES = 128
BF16_SUBLANES = 16
LOG2E = 1.4426950408889634
FLASH_ROWS = 64
VMEM_LIMIT = 56 * 1024 * 1024


def _cparams(sem):
    return pltpu.CompilerParams(dimension_semantics=sem, vmem_limit_bytes=VMEM_LIMIT)


def _sigmoid(x):
    return 1.0 / (1.0 + jnp.exp(-x))


def _gelu(x):
    return 0.5 * x * (1.0 + jnp.tanh(0.7978845608028654 * (x + 0.044715 * (x * x * x))))


def _dot(a, b):
    return jnp.dot(a, b, preferred_element_type=F32)


def _dot_nt(a, b):
    return lax.dot_general(a, b, (((1,), (1,)), ((), ())), preferred_element_type=F32)


def _dot_tn(a, b):
    return lax.dot_general(a, b, (((0,), (0,)), ((), ())), preferred_element_type=F32)


def _adaln_kernel(cv_ref, down_ref, up_ref, bias_ref, o_ref, hd_ref):
    @pl.when(pl.program_id(1) == 0)
    def _():
        cv = cv_ref[...]
        hd_ref[...] = _dot(cv * _sigmoid(cv), down_ref[...])

    o_ref[...] = _dot(hd_ref[...], up_ref[...]) + bias_ref[...]


def adaln_all(cv, down, up, bias, tn=2048):
    depth, d, rank = down.shape
    nout = up.shape[-1]
    rows = cv.shape[0]
    return pl.pallas_call(
        _adaln_kernel,
        grid=(depth, nout // tn),
        in_specs=[
            pl.BlockSpec((rows, d), lambda l, j: (0, 0)),
            pl.BlockSpec((None, d, rank), lambda l, j: (l, 0, 0)),
            pl.BlockSpec((None, rank, tn), lambda l, j: (l, 0, j)),
            pl.BlockSpec((None, 1, tn), lambda l, j: (l, 0, j)),
        ],
        out_specs=pl.BlockSpec((None, rows, tn), lambda l, j: (l, 0, j)),
        out_shape=jax.ShapeDtypeStruct((depth, rows, nout), F32),
        scratch_shapes=[pltpu.VMEM((rows, rank), F32)],
        compiler_params=_cparams(("arbitrary", "arbitrary")),
        name="adaln",
    )(cv, down, up, bias.reshape(depth, 1, nout))


def _norm_mod_kernel(x_ref, gain_ref, scale_ref, shift_ref, o_ref):
    x = x_ref[...]
    ms = jnp.mean(x * x, axis=-1, keepdims=True)
    xn = x * lax.rsqrt(ms + EPS) * gain_ref[...]
    o_ref[...] = (xn * (1.0 + scale_ref[...]) + shift_ref[...]).astype(o_ref.dtype)


def norm_mod(x, gain, scale, shift, tm=512):
    m, d = x.shape
    nb = scale.shape[0]
    assert (m // tm) % nb == 0
    tpb = (m // tm) // nb
    vec = pl.BlockSpec((None, 1, d), lambda i: (i // tpb, 0, 0))
    return pl.pallas_call(
        _norm_mod_kernel,
        grid=(m // tm,),
        in_specs=[pl.BlockSpec((tm, d), lambda i: (i, 0)),
                  pl.BlockSpec((1, d), lambda i: (0, 0)), vec, vec],
        out_specs=pl.BlockSpec((tm, d), lambda i: (i, 0)),
        out_shape=jax.ShapeDtypeStruct((m, d), BF16),
        compiler_params=_cparams(("parallel",)),
        name="norm_mod",
    )(x, gain.reshape(1, d), scale, shift)


def _mm_store_kernel(a_ref, w_ref, o_ref):
    o_ref[...] = _dot(a_ref[...], w_ref[...]).astype(o_ref.dtype)


def matmul(a, w, layer, out_dtype, tm, tn):
    m, k = a.shape
    n = w.shape[2]
    return pl.pallas_call(
        _mm_store_kernel,
        grid=(m // tm, n // tn),
        in_specs=[pl.BlockSpec((tm, k), lambda i, j: (i, 0)),
                  pl.BlockSpec((None, k, tn), lambda i, j: (layer, 0, j))],
        out_specs=pl.BlockSpec((tm, tn), lambda i, j: (i, j)),
        out_shape=jax.ShapeDtypeStruct((m, n), out_dtype),
        compiler_params=_cparams(("parallel", "arbitrary")),
        name="matmul",
    )(a, w)


def _cast_jobs(casts, steps_i, steps_j):
    in_specs, out_specs, out_shapes, args = [], [], [], []
    for w, layer in casts:
        _, k, c = w.shape
        rows = next(r for r in range(BF16_SUBLANES, k + 1, BF16_SUBLANES)
                    if k % r == 0 and k // r <= steps_i * steps_j)
        last = k // rows - 1
        in_specs.append(pl.BlockSpec(
            (None, rows, c),
            lambda i, j, l=layer, e=last: (l, jnp.minimum(i * steps_j + j, e), 0)))
        out_specs.append(pl.BlockSpec(
            (None, rows, c), lambda i, j, e=last: (0, jnp.minimum(i * steps_j + j, e), 0)))
        out_shapes.append(jax.ShapeDtypeStruct((1, k, c), BF16))
        args.append(w)
    return in_specs, out_specs, out_shapes, args


def _run_casts(src_refs, dst_refs):
    for src, dst in zip(src_refs, dst_refs):
        dst[...] = src[...].astype(dst.dtype)


def _mm_resid_kernel(*refs, n_a, n_c):
    a_refs, w_refs = refs[:n_a], refs[n_a:2 * n_a]
    resid_ref, gate_ref = refs[2 * n_a:2 * n_a + 2]
    cast_src = refs[2 * n_a + 2:2 * n_a + 2 + n_c]
    o_ref = refs[2 * n_a + 2 + n_c]
    cast_dst = refs[2 * n_a + 3 + n_c:]
    acc = _dot(a_refs[0][...], w_refs[0][...])
    for a_ref, w_ref in zip(a_refs[1:], w_refs[1:]):
        acc = acc + _dot(a_ref[...], w_ref[...])
    o_ref[...] = resid_ref[...] + gate_ref[...] * acc
    _run_casts(cast_src, cast_dst)


def matmul_resid(a_list, w, layer, resid, gate, tm, tn, casts=()):
    m, n = resid.shape
    nb = gate.shape[0]
    assert (m // tm) % nb == 0
    tpb = (m // tm) // nb
    n_a = len(a_list)
    c_in, c_out, c_shapes, c_args = _cast_jobs(casts, m // tm, n // tn)
    in_specs, w_args, off = [], [], 0
    for a in a_list:
        in_specs.append(pl.BlockSpec((tm, a.shape[1]), lambda i, j: (i, 0)))
    for a in a_list:
        ka = a.shape[1]
        assert off % ka == 0
        in_specs.append(pl.BlockSpec((None, ka, tn), lambda i, j, r=off // ka: (layer, r, j)))
        w_args.append(w)
        off += ka
    assert off == w.shape[1]
    in_specs += [pl.BlockSpec((tm, tn), lambda i, j: (i, j)),
                 pl.BlockSpec((None, 1, tn), lambda i, j: (i // tpb, 0, j))]
    outs = pl.pallas_call(
        functools.partial(_mm_resid_kernel, n_a=n_a, n_c=len(casts)),
        grid=(m // tm, n // tn),
        in_specs=in_specs + c_in,
        out_specs=[pl.BlockSpec((tm, tn), lambda i, j: (i, j))] + c_out,
        out_shape=[jax.ShapeDtypeStruct((m, n), F32)] + c_shapes,
        compiler_params=_cparams(("arbitrary", "arbitrary")),
        name="matmul_resid",
    )(*a_list, *w_args, resid, gate, *c_args)
    return outs[0], outs[1:]


def _mm_swiglu_kernel(a_ref, wg_ref, wu_ref, *refs, n_c):
    cast_src, o_ref, cast_dst = refs[:n_c], refs[n_c], refs[n_c + 1:]
    a = a_ref[...]
    g = _dot(a, wg_ref[...])
    u = _dot(a, wu_ref[...])
    o_ref[...] = (g * _sigmoid(g) * u).astype(o_ref.dtype)
    _run_casts(cast_src, cast_dst)


def matmul_swiglu(a, wg, wu, layer, tm, tn, casts=()):
    m, k = a.shape
    n = wg.shape[2]
    wspec = pl.BlockSpec((None, k, tn), lambda i, j: (layer, 0, j))
    c_in, c_out, c_shapes, c_args = _cast_jobs(casts, m // tm, n // tn)
    outs = pl.pallas_call(
        functools.partial(_mm_swiglu_kernel, n_c=len(casts)),
        grid=(m // tm, n // tn),
        in_specs=[pl.BlockSpec((tm, k), lambda i, j: (i, 0)), wspec, wspec] + c_in,
        out_specs=[pl.BlockSpec((tm, tn), lambda i, j: (i, j))] + c_out,
        out_shape=[jax.ShapeDtypeStruct((m, n), BF16)] + c_shapes,
        compiler_params=_cparams(("arbitrary", "arbitrary")),
        name="matmul_swiglu",
    )(a, wg, wu, *c_args)
    return outs[0], outs[1:]


def _rope_slab(x, cos, sin_signed, even):
    swapped = jnp.where(even, pltpu.roll(x, LANES - 1, 1), pltpu.roll(x, 1, 1))
    return x * cos + swapped * sin_signed


def _even_lanes(rows):
    return (lax.broadcasted_iota(jnp.int32, (rows, LANES), 1) % 2) == 0


def _sgu_kernel(u_ref, v_ref, gain_ref, w_ref, b_ref, o_ref):
    tm = u_ref.shape[0]
    groups = w_ref.shape[0]
    v = _gelu(v_ref[...].astype(F32))
    ms = jnp.mean(v * v, axis=-1, keepdims=True)
    vn = (v * lax.rsqrt(ms + EPS) * gain_ref[...]).astype(BF16)
    bias = b_ref[...]
    for c in range(tm // SGU_CHUNK):
        rows = slice(c * SGU_CHUNK, (c + 1) * SGU_CHUNK)
        for g in range(groups):
            cols = slice(g * SGU_GROUP, (g + 1) * SGU_GROUP)
            mixed = _dot(w_ref[g], vn[rows, cols]) + bias[:, g:g + 1]
            u = _gelu(u_ref[rows, cols].astype(F32))
            o_ref[rows, cols] = (u * mixed).astype(o_ref.dtype)


def sgu(p, gain, w_s, b_s, tm=256):
    m = p.shape[0]
    groups = w_s.shape[0]
    width = groups * SGU_GROUP
    return pl.pallas_call(
        _sgu_kernel,
        grid=(m // tm,),
        in_specs=[pl.BlockSpec((tm, width), lambda i: (i, 0)),
                  pl.BlockSpec((tm, width), lambda i: (i, 1)),
                  pl.BlockSpec((1, width), lambda i: (0, 0)),
                  pl.BlockSpec((groups, SGU_CHUNK, SGU_CHUNK), lambda i: (0, 0, 0)),
                  pl.BlockSpec((SGU_CHUNK, groups), lambda i: (0, 0))],
        out_specs=pl.BlockSpec((tm, width), lambda i: (i, 0)),
        out_shape=jax.ShapeDtypeStruct((m, width), BF16),
        compiler_params=_cparams(("parallel",)),
        name="sgu",
    )(p, p, gain.reshape(1, width), w_s.astype(BF16), b_s.T)


def _qkv_prep_kernel(q_ref, k_ref, v_ref, qg_ref, kg_ref, cos_ref, sin_ref,
                     qo_ref, ko_ref, vo_ref, *, rope, q_scale):
    tm = q_ref.shape[0]
    even = _even_lanes(tm)
    cos = cos_ref[...]
    sin = sin_ref[...]

    def prep(x, gain):
        ms = jnp.mean(x * x, axis=-1, keepdims=True)
        xn = x * lax.rsqrt(ms + EPS) * gain
        if rope:
            xn = _rope_slab(xn, cos, sin, even)
        return xn

    qg = qg_ref[...]
    kg = kg_ref[...]
    for h in range(q_ref.shape[1] // HEAD_DIM):
        cols = slice(h * HEAD_DIM, (h + 1) * HEAD_DIM)
        qo_ref[:, cols] = (prep(q_ref[:, cols].astype(F32), qg) * q_scale).astype(qo_ref.dtype)
    for h in range(k_ref.shape[1] // HEAD_DIM):
        cols = slice(h * HEAD_DIM, (h + 1) * HEAD_DIM)
        ko_ref[:, cols] = prep(k_ref[:, cols].astype(F32), kg).astype(ko_ref.dtype)
    vo_ref[...] = v_ref[...].astype(vo_ref.dtype)


def qkv_prep(p, q_gain, k_gain, cos, sin, col0, n_q, n_kv, rope, tm=256):
    m = p.shape[0]
    qw, kw = n_q * HEAD_DIM, n_kv * HEAD_DIM
    assert col0 % qw == 0 and (col0 + qw) % kw == 0
    pos_tiles = cos.shape[0] // tm
    tab = pl.BlockSpec((tm, HEAD_DIM), lambda i: (i % pos_tiles, 0))
    kern = functools.partial(_qkv_prep_kernel, rope=rope, q_scale=HEAD_DIM ** -0.5 * LOG2E)
    return pl.pallas_call(
        kern,
        grid=(m // tm,),
        in_specs=[pl.BlockSpec((tm, qw), lambda i: (i, col0 // qw)),
                  pl.BlockSpec((tm, kw), lambda i: (i, (col0 + qw) // kw)),
                  pl.BlockSpec((tm, kw), lambda i: (i, (col0 + qw) // kw + 1)),
                  pl.BlockSpec((1, HEAD_DIM), lambda i: (0, 0)),
                  pl.BlockSpec((1, HEAD_DIM), lambda i: (0, 0)),
                  tab, tab],
        out_specs=[pl.BlockSpec((tm, qw), lambda i: (i, 0)),
                   pl.BlockSpec((tm, kw), lambda i: (i, 0)),
                   pl.BlockSpec((tm, kw), lambda i: (i, 0))],
        out_shape=[jax.ShapeDtypeStruct((m, qw), BF16),
                   jax.ShapeDtypeStruct((m, kw), BF16),
                   jax.ShapeDtypeStruct((m, kw), BF16)],
        compiler_params=_cparams(("parallel",)),
        name="qkv_prep",
    )(p, p, p, q_gain.reshape(1, HEAD_DIM), k_gain.reshape(1, HEAD_DIM), cos, sin)


def _flash_kernel(q_ref, k_ref, v_ref, o_ref, sa_ref, sb_ref, pa_ref, pb_ref,
                  m_ref, l_ref, acc_ref, *, tk, group):
    tq = q_ref.shape[0]
    nk = k_ref.shape[0] // tk
    ncb = tk // LANES
    q = jnp.concatenate(
        [q_ref[:, g * HEAD_DIM:(g + 1) * HEAD_DIM] for g in range(group)], axis=0)
    m_ref[...] = jnp.full(m_ref.shape, -jnp.inf, F32)
    l_ref[...] = jnp.zeros(l_ref.shape, F32)
    acc_ref[...] = jnp.zeros(acc_ref.shape, F32)

    def scores(j, s_ref):
        off = pl.multiple_of(j * tk, tk)
        s_ref[...] = _dot_nt(q, k_ref[pl.ds(off, tk), :])

    def softmax_pv(j, s_ref, p_ref):
        off = pl.multiple_of(j * tk, tk)
        for r0 in range(0, group * tq, FLASH_ROWS):
            rows = slice(r0, r0 + FLASH_ROWS)
            blocks = [s_ref[rows, cb * LANES:(cb + 1) * LANES] for cb in range(ncb)]
            mx = blocks[0]
            for blk in blocks[1:]:
                mx = jnp.maximum(mx, blk)
            m_prev = m_ref[rows, :]
            m_new = jnp.maximum(
                m_prev, jnp.broadcast_to(jnp.max(mx, axis=-1, keepdims=True), mx.shape))
            alpha = jnp.exp2(m_prev - m_new)
            psum = None
            for cb, blk in enumerate(blocks):
                pb = jnp.exp2(blk - m_new)
                psum = pb if psum is None else psum + pb
                p_ref[rows, cb * LANES:(cb + 1) * LANES] = pb.astype(BF16)
            l_ref[rows, :] = alpha * l_ref[rows, :] + psum
            acc_ref[rows, :] = alpha * acc_ref[rows, :]
            m_ref[rows, :] = m_new
        acc_ref[...] += _dot(p_ref[...], v_ref[pl.ds(off, tk), :])

    scores(0, sa_ref)

    def body(jj, carry):
        j = 2 * jj
        scores(j + 1, sb_ref)
        softmax_pv(j, sa_ref, pa_ref)
        scores(j + 2, sa_ref)
        softmax_pv(j + 1, sb_ref, pb_ref)
        return carry

    lax.fori_loop(0, (nk - 1) // 2, body, 0)
    if nk % 2 == 1:
        softmax_pv(nk - 1, sa_ref, pa_ref)
    else:
        scores(nk - 1, sb_ref)
        softmax_pv(nk - 2, sa_ref, pa_ref)
        softmax_pv(nk - 1, sb_ref, pb_ref)

    out = acc_ref[...] / jnp.sum(l_ref[...], axis=-1, keepdims=True)
    for g in range(group):
        o_ref[:, g * HEAD_DIM:(g + 1) * HEAD_DIM] = out[g * tq:(g + 1) * tq].astype(o_ref.dtype)


def flash_attention(q, k, v, tq, tk):
    b, n, qw = q.shape
    s = k.shape[1]
    kvh = k.shape[2] // HEAD_DIM
    group = qw // HEAD_DIM // kvh
    gw = group * HEAD_DIM
    assert s % tk == 0 and n % tq == 0
    return pl.pallas_call(
        functools.partial(_flash_kernel, tk=tk, group=group),
        grid=(b, kvh, n // tq),
        in_specs=[pl.BlockSpec((None, tq, gw), lambda bb, h, i: (bb, i, h)),
                  pl.BlockSpec((None, s, HEAD_DIM), lambda bb, h, i: (bb, 0, h)),
                  pl.BlockSpec((None, s, HEAD_DIM), lambda bb, h, i: (bb, 0, h))],
        out_specs=pl.BlockSpec((None, tq, gw), lambda bb, h, i: (bb, i, h)),
        out_shape=jax.ShapeDtypeStruct((b, n, qw), BF16),
        scratch_shapes=[pltpu.VMEM((group * tq, tk), F32),
                        pltpu.VMEM((group * tq, tk), F32),
                        pltpu.VMEM((group * tq, tk), BF16),
                        pltpu.VMEM((group * tq, tk), BF16),
                        pltpu.VMEM((group * tq, LANES), F32),
                        pltpu.VMEM((group * tq, LANES), F32),
                        pltpu.VMEM((group * tq, HEAD_DIM), F32)],
        compiler_params=_cparams(("parallel", "parallel", "arbitrary")),
        name="flash_attention",
    )(q, k, v)


def _ret_prep_kernel(q_ref, k_ref, cos_ref, sin_ref, qo_ref, ko_ref, *, rope, k_scale):
    tm = q_ref.shape[0]
    even = _even_lanes(tm)
    slabs_per_head = RET_DIM // LANES
    for sl in range(q_ref.shape[1] // LANES):
        cols = slice(sl * LANES, (sl + 1) * LANES)
        tcols = slice((sl % slabs_per_head) * LANES, (sl % slabs_per_head + 1) * LANES)
        q = q_ref[:, cols].astype(F32)
        k = k_ref[:, cols].astype(F32) * k_scale
        if rope:
            cos = cos_ref[:, tcols]
            sin = sin_ref[:, tcols]
            q = _rope_slab(q, cos, sin, even)
            k = _rope_slab(k, cos, sin, even)
        qo_ref[:, cols] = q.astype(qo_ref.dtype)
        ko_ref[:, cols] = k.astype(ko_ref.dtype)


def ret_prep(p, cos, sin, width, rope, tm=256):
    m = p.shape[0]
    pos_tiles = cos.shape[0] // tm
    tab = pl.BlockSpec((tm, RET_DIM), lambda i: (i % pos_tiles, 0))
    out = jax.ShapeDtypeStruct((m, width), BF16)
    ospec = pl.BlockSpec((tm, width), lambda i: (i, 0))
    return pl.pallas_call(
        functools.partial(_ret_prep_kernel, rope=rope, k_scale=RET_DIM ** -0.5),
        grid=(m // tm,),
        in_specs=[pl.BlockSpec((tm, width), lambda i: (i, 0)),
                  pl.BlockSpec((tm, width), lambda i: (i, 1)),
                  tab, tab],
        out_specs=[ospec, ospec],
        out_shape=[out, out],
        compiler_params=_cparams(("parallel",)),
        name="ret_prep",
    )(p, p, cos, sin)


def _ret_kernel(dec_ref, qf_ref, kf_ref, vf_ref, qb_ref, kb_ref, vb_ref, s0_ref,
                of_ref, ob_ref, sfin_ref, st_ref, dm_ref, rd_ref, *, hb):
    c = RET_CHUNK
    s = pl.program_id(2)

    @pl.when(s == 0)
    def _():
        st_ref[...] = s0_ref[...]
        row = lax.broadcasted_iota(jnp.int32, (c, c), 0).astype(F32)
        col = lax.broadcasted_iota(jnp.int32, (c, c), 1).astype(F32)
        for d in range(2):
            for hh in range(hb):
                ld = -jnp.exp(dec_ref[d, hh])
                if d == 0:
                    dist, keep = row - col, row >= col
                    q_pow, k_pow = row + 1.0, (c - 1.0) - row
                else:
                    dist, keep = col - row, col > row
                    q_pow, k_pow = c - row, row
                dm_ref[d, hh] = jnp.where(keep, jnp.exp(ld * jnp.maximum(dist, 0.0)), 0.0)
                rd_ref[d, hh, 0] = jnp.exp(ld * q_pow)
                rd_ref[d, hh, 1] = jnp.exp(ld * k_pow)

    streams = ((qf_ref, kf_ref, vf_ref, of_ref), (qb_ref, kb_ref, vb_ref, ob_ref))
    for d, (q_ref, k_ref, v_ref, o_ref) in enumerate(streams):
        for hh in range(hb):
            cols = slice(hh * RET_DIM, (hh + 1) * RET_DIM)
            q = q_ref[:, cols]
            k = k_ref[:, cols]
            v = v_ref[:, cols]
            state = st_ref[d, hh]
            att = _dot_nt(q, k) * dm_ref[d, hh]
            o = _dot(att.astype(BF16), v) + rd_ref[d, hh, 0] * _dot(q, state.astype(BF16))
            kt = (k.astype(F32) * rd_ref[d, hh, 1]).astype(BF16)
            chunk_decay = jnp.exp(-jnp.exp(dec_ref[d, hh]) * float(c))
            st_ref[d, hh] = state * chunk_decay + _dot_tn(kt, v)
            o_ref[:, cols] = o

    @pl.when(s == pl.num_programs(2) - 1)
    def _():
        sfin_ref[...] = st_ref[...]


def retention(q, k, p, dec, s0, hb=4):
    b, n, w = q.shape
    h = w // RET_DIM
    c = RET_CHUNK
    nc = n // c
    v0 = 2 * (h // hb)
    fw = pl.BlockSpec((None, c, hb * RET_DIM), lambda bb, g, s: (bb, s, g))
    bw = pl.BlockSpec((None, c, hb * RET_DIM), lambda bb, g, s: (bb, nc - 1 - s, g))
    vfw = pl.BlockSpec((None, c, hb * RET_DIM), lambda bb, g, s: (bb, s, v0 + g))
    vbw = pl.BlockSpec((None, c, hb * RET_DIM), lambda bb, g, s: (bb, nc - 1 - s, v0 + g))
    st = pl.BlockSpec((None, 2, hb, RET_DIM, RET_DIM), lambda bb, g, s: (bb, 0, g, 0, 0))
    return pl.pallas_call(
        functools.partial(_ret_kernel, hb=hb),
        grid=(b, h // hb, nc),
        in_specs=[pl.BlockSpec((2, hb, 1, RET_DIM), lambda bb, g, s: (0, g, 0, 0)),
                  fw, fw, vfw, bw, bw, vbw, st],
        out_specs=[fw, bw, st],
        out_shape=[jax.ShapeDtypeStruct((b, n, w), F32),
                   jax.ShapeDtypeStruct((b, n, w), F32),
                   jax.ShapeDtypeStruct(s0.shape, F32)],
        scratch_shapes=[pltpu.VMEM((2, hb, RET_DIM, RET_DIM), F32),
                        pltpu.VMEM((2, hb, c, c), F32),
                        pltpu.VMEM((2, hb, 2, c, RET_DIM), F32)],
        compiler_params=_cparams(("parallel", "parallel", "arbitrary")),
        name="retention",
    )(dec, q, k, p, q, k, p, s0)


def _ret_out_kernel(of_ref, ob_ref, g_ref, o_ref):
    for h in range(of_ref.shape[1] // RET_DIM):
        cols = slice(h * RET_DIM, (h + 1) * RET_DIM)
        o = of_ref[:, cols] + ob_ref[:, cols]
        ms = jnp.mean(o * o, axis=-1, keepdims=True)
        g = g_ref[:, cols].astype(F32)
        o_ref[:, cols] = (g * _sigmoid(g) * (o * lax.rsqrt(ms + EPS))).astype(o_ref.dtype)


def ret_out(o_f, o_b, p, tm=256):
    m, w = o_f.shape
    spec = pl.BlockSpec((tm, w), lambda i: (i, 0))
    return pl.pallas_call(
        _ret_out_kernel,
        grid=(m // tm,),
        in_specs=[spec, spec, pl.BlockSpec((tm, w), lambda i: (i, 3))],
        out_specs=spec,
        out_shape=jax.ShapeDtypeStruct((m, w), BF16),
        compiler_params=_cparams(("parallel",)),
        name="ret_out",
    )(o_f, o_b, p)


def _rope_tables(rows, head_dim):
    row = jnp.repeat(jnp.arange(rows), GRID_W).astype(F32)
    col = jnp.tile(jnp.arange(GRID_W), rows).astype(F32)
    n_freq = head_dim // 4
    inv_freq = ROPE_THETA ** (-jnp.arange(n_freq, dtype=F32) / n_freq)
    ang = jnp.concatenate([row[:, None] * inv_freq, col[:, None] * inv_freq], axis=-1)
    sign = jnp.tile(jnp.array([-1.0, 1.0], F32), head_dim // 2)
    return (jnp.repeat(jnp.cos(ang), 2, axis=-1),
            jnp.repeat(jnp.sin(ang), 2, axis=-1) * sign)


def _row_tile(m, pref):
    return pref if m % pref == 0 else m


def _ffn(h, mods, norm_gain, wg, wu, wd, gate_up_casts=(), down_casts=()):
    sh2, sc2, g2 = mods
    m = h.shape[0]
    hn = norm_mod(h, norm_gain, sc2, sh2)
    act, cast1 = matmul_swiglu(hn, wg, wu, 0, tm=_row_tile(m, 2048), tn=256,
                               casts=gate_up_casts)
    out, cast2 = matmul_resid([act], wd, 0, h, g2, tm=512, tn=512, casts=down_casts)
    return out, cast1, cast2


def kernel(x, c, ctx, c_ctx, norm_mix, norm_ffn, mod_down, mod_up, mod_bias,
           ffn_gate, ffn_up, ffn_down, ev_w_in, ev_w_out, sgu_norm, sgu_w, sgu_b,
           q_norm, k_norm, ret_w_in, ret_w_out, ret_decay_fwd, ret_decay_bwd):
    b, n, d = x.shape
    n_ctx = ctx.shape[1]
    depth = norm_mix.shape[0]
    a_width = sgu_w.shape[1] * SGU_GROUP
    n_q = (ev_w_out.shape[1] - a_width) // HEAD_DIM
    ret_w = ret_w_out.shape[1]
    ret_heads = ret_w // RET_DIM

    cv = jnp.concatenate([c, c_ctx[None], jnp.zeros((8 - b - 1, d), F32)], axis=0)
    mods = adaln_all(cv, mod_down, mod_up, mod_bias)

    cos_a, sin_a = _rope_tables(n // GRID_W, HEAD_DIM)
    cos_r, sin_r = _rope_tables(n // GRID_W, RET_DIM)

    def first(w):
        return w[:1].astype(BF16)

    mix_in, mix_out = first(ev_w_in), first(ev_w_out)
    wg, wu, wd = first(ffn_gate), first(ffn_up), first(ffn_down)

    h_lat = x.reshape(b * n, d)
    h_ctx = ctx.reshape(b * n_ctx, d)
    for i in range(depth):
        last = i == depth - 1
        j = i // 2
        lat_mod = [mods[i, :b, t * d:(t + 1) * d].reshape(b, 1, d) for t in range(N_MOD)]
        ctx_mod = [mods[i, b:b + 1, t * d:(t + 1) * d].reshape(1, 1, d) for t in range(N_MOD)]
        xn = norm_mod(h_lat, norm_mix[i], lat_mod[1], lat_mod[0])
        cn = norm_mod(h_ctx, norm_mix[i], ctx_mod[1], ctx_mod[0])
        if i % 2 == 0:
            p_l = matmul(xn, mix_in, 0, BF16, tm=1024, tn=512)
            p_c = matmul(cn, mix_in, 0, BF16, tm=b * n_ctx, tn=512)
            ql, kl, vl = qkv_prep(p_l, q_norm[j], k_norm[j], cos_a, sin_a,
                                  2 * a_width, n_q, KV_HEADS, rope=True)
            qc, kc, vc = qkv_prep(p_c, q_norm[j], k_norm[j], cos_a, sin_a,
                                  2 * a_width, n_q, KV_HEADS, rope=False)
            kw = KV_HEADS * HEAD_DIM
            k_all = jnp.concatenate([kc.reshape(b, n_ctx, kw), kl.reshape(b, n, kw)], axis=1)
            v_all = jnp.concatenate([vc.reshape(b, n_ctx, kw), vl.reshape(b, n, kw)], axis=1)
            tk = next(t for t in (768, 512, 256) if (n_ctx + n) % t == 0)
            att_l = flash_attention(ql.reshape(b, n, -1), k_all, v_all, tq=256, tk=tk)
            sgu_l = sgu(p_l, sgu_norm[j], sgu_w[j], sgu_b[j])
            h_lat, _ = matmul_resid([sgu_l, att_l.reshape(b * n, -1)], mix_out, 0, h_lat,
                                    lat_mod[2], tm=1024, tn=512)
            if not last:
                att_c = flash_attention(qc.reshape(b, n_ctx, -1), kc.reshape(b, n_ctx, kw),
                                        vc.reshape(b, n_ctx, kw), tq=n_ctx, tk=n_ctx)
                sgu_c = sgu(p_c, sgu_norm[j], sgu_w[j], sgu_b[j])
                h_ctx, _ = matmul_resid([sgu_c, att_c.reshape(b * n_ctx, -1)], mix_out, 0,
                                        h_ctx, ctx_mod[2], tm=b * n_ctx, tn=512)
        else:
            p_l = matmul(xn, mix_in, 0, BF16, tm=1024, tn=512)
            p_c = matmul(cn, mix_in, 0, BF16, tm=b * n_ctx, tn=512)
            ql, kl = ret_prep(p_l, cos_r, sin_r, ret_w, rope=True)
            qc, kc = ret_prep(p_c, cos_r, sin_r, ret_w, rope=False)
            dec = jnp.stack([ret_decay_fwd[j], ret_decay_bwd[j]]).astype(F32)
            dec = jnp.broadcast_to(dec[:, :, None, None], (2, ret_heads, 1, RET_DIM))
            s0 = jnp.zeros((b, 2, ret_heads, RET_DIM, RET_DIM), F32)
            shp_c, shp_l = (b, n_ctx, ret_w), (b, n, ret_w)
            ocf, ocb, s_ctx = retention(qc.reshape(shp_c), kc.reshape(shp_c),
                                        p_c.reshape(b, n_ctx, -1), dec, s0)
            olf, olb, _ = retention(ql.reshape(shp_l), kl.reshape(shp_l),
                                    p_l.reshape(b, n, -1), dec, s_ctx)
            y_l = ret_out(olf.reshape(b * n, ret_w), olb.reshape(b * n, ret_w), p_l)
            h_lat, _ = matmul_resid([y_l], mix_out, 0, h_lat, lat_mod[2], tm=1024, tn=512)
            if not last:
                y_c = ret_out(ocf.reshape(b * n_ctx, ret_w), ocb.reshape(b * n_ctx, ret_w), p_c)
                h_ctx, _ = matmul_resid([y_c], mix_out, 0, h_ctx, ctx_mod[2],
                                        tm=b * n_ctx, tn=512)

        if last:
            h_lat, _, _ = _ffn(h_lat, lat_mod[3:], norm_ffn[i], wg, wu, wd)
        else:
            nj = (i + 1) // 2
            nxt_in, nxt_out = (ev_w_in, ev_w_out) if (i + 1) % 2 == 0 else (ret_w_in, ret_w_out)
            h_lat, cast1, cast2 = _ffn(
                h_lat, lat_mod[3:], norm_ffn[i], wg, wu, wd,
                gate_up_casts=[(ffn_down, i + 1), (ffn_gate, i + 1), (nxt_in, nj)],
                down_casts=[(ffn_up, i + 1), (nxt_out, nj)])
            h_ctx, _, _ = _ffn(h_ctx, ctx_mod[3:], norm_ffn[i], wg, wu, wd)
            (wd, wg, mix_in), (wu, mix_out) = cast1, cast2
    return h_lat.reshape(b, n, d)
```

```python
import functools

import jax
import jax.numpy as jnp
from jax import lax
from jax.experimental import pallas as pl
from jax.experimental.pallas import tpu as pltpu

F32 = jnp.float32
BF16 = jnp.bfloat16

EPS = 1e-6
GRID_W = 64
N_MOD = 6
ROPE_THETA = 10000.0
SGU_CHUNK = 128
SGU_GROUP = 128
HEAD_DIM = 128
KV_HEADS = 4
RET_DIM = 256
RET_CHUNK = 256
LANES = 128
BF16_SUBLANES = 16
LOG2E = 1.4426950408889634
FLASH_ROWS = 64
VMEM_LIMIT = 56 * 1024 * 1024


def _cparams(sem):
    return pltpu.CompilerParams(dimension_semantics=sem, vmem_limit_bytes=VMEM_LIMIT)


def _sigmoid(x):
    return 1.0 / (1.0 + jnp.exp(-x))


def _gelu(x):
    return 0.5 * x * (1.0 + jnp.tanh(0.7978845608028654 * (x + 0.044715 * (x * x * x))))


def _dot(a, b):
    return jnp.dot(a, b, preferred_element_type=F32)


def _dot_nt(a, b):
    return lax.dot_general(a, b, (((1,), (1,)), ((), ())), preferred_element_type=F32)


def _dot_tn(a, b):
    return lax.dot_general(a, b, (((0,), (0,)), ((), ())), preferred_element_type=F32)


def _adaln_kernel(cv_ref, down_ref, up_ref, bias_ref, o_ref, hd_ref):
    @pl.when(pl.program_id(1) == 0)
    def _():
        cv = cv_ref[...]
        hd_ref[...] = _dot(cv * _sigmoid(cv), down_ref[...])

    o_ref[...] = _dot(hd_ref[...], up_ref[...]) + bias_ref[...]


def adaln_all(cv, down, up, bias, tn=2048):
    depth, d, rank = down.shape
    nout = up.shape[-1]
    rows = cv.shape[0]
    return pl.pallas_call(
        _adaln_kernel,
        grid=(depth, nout // tn),
        in_specs=[
            pl.BlockSpec((rows, d), lambda l, j: (0, 0)),
            pl.BlockSpec((None, d, rank), lambda l, j: (l, 0, 0)),
            pl.BlockSpec((None, rank, tn), lambda l, j: (l, 0, j)),
            pl.BlockSpec((None, 1, tn), lambda l, j: (l, 0, j)),
        ],
        out_specs=pl.BlockSpec((None, rows, tn), lambda l, j: (l, 0, j)),
        out_shape=jax.ShapeDtypeStruct((depth, rows, nout), F32),
        scratch_shapes=[pltpu.VMEM((rows, rank), F32)],
        compiler_params=_cparams(("arbitrary", "arbitrary")),
        name="adaln",
    )(cv, down, up, bias.reshape(depth, 1, nout))


def _norm_mod_kernel(x_ref, gain_ref, scale_ref, shift_ref, o_ref):
    x = x_ref[...]
    ms = jnp.mean(x * x, axis=-1, keepdims=True)
    xn = x * lax.rsqrt(ms + EPS) * gain_ref[...]
    o_ref[...] = (xn * (1.0 + scale_ref[...]) + shift_ref[...]).astype(o_ref.dtype)


def norm_mod(x, gain, scale, shift, tm=512):
    m, d = x.shape
    nb = scale.shape[0]
    assert (m // tm) % nb == 0
    tpb = (m // tm) // nb
    vec = pl.BlockSpec((None, 1, d), lambda i: (i // tpb, 0, 0))
    return pl.pallas_call(
        _norm_mod_kernel,
        grid=(m // tm,),
        in_specs=[pl.BlockSpec((tm, d), lambda i: (i, 0)),
                  pl.BlockSpec((1, d), lambda i: (0, 0)), vec, vec],
        out_specs=pl.BlockSpec((tm, d), lambda i: (i, 0)),
        out_shape=jax.ShapeDtypeStruct((m, d), BF16),
        compiler_params=_cparams(("parallel",)),
        name="norm_mod",
    )(x, gain.reshape(1, d), scale, shift)


def _mm_store_kernel(a_ref, w_ref, *refs, n_c):
    cast_src, o_ref, cast_dst = refs[:n_c], refs[n_c], refs[n_c + 1:]
    o_ref[...] = _dot(a_ref[...], w_ref[...]).astype(o_ref.dtype)
    _run_casts(cast_src, cast_dst)


def matmul(a, w, layer, out_dtype, tm, tn, casts=()):
    m, k = a.shape
    n = w.shape[2]
    c_in, c_out, c_shapes, c_args = _cast_jobs(casts, m // tm, n // tn)
    outs = pl.pallas_call(
        functools.partial(_mm_store_kernel, n_c=len(casts)),
        grid=(m // tm, n // tn),
        in_specs=[pl.BlockSpec((tm, k), lambda i, j: (i, 0)),
                  pl.BlockSpec((None, k, tn), lambda i, j: (layer, 0, j))] + c_in,
        out_specs=[pl.BlockSpec((tm, tn), lambda i, j: (i, j))] + c_out,
        out_shape=[jax.ShapeDtypeStruct((m, n), out_dtype)] + c_shapes,
        compiler_params=_cparams(("arbitrary", "arbitrary")),
        name="matmul",
    )(a, w, *c_args)
    return outs[0], outs[1:]


def _cast_jobs(casts, steps_i, steps_j):
    in_specs, out_specs, out_shapes, args = [], [], [], []
    for w, layer in casts:
        _, k, c = w.shape
        rows = next(r for r in range(BF16_SUBLANES, k + 1, BF16_SUBLANES)
                    if k % r == 0 and k // r <= steps_i * steps_j)
        last = k // rows - 1
        in_specs.append(pl.BlockSpec(
            (None, rows, c),
            lambda i, j, l=layer, e=last: (l, jnp.minimum(i * steps_j + j, e), 0)))
        out_specs.append(pl.BlockSpec(
            (None, rows, c), lambda i, j, e=last: (0, jnp.minimum(i * steps_j + j, e), 0)))
        out_shapes.append(jax.ShapeDtypeStruct((1, k, c), BF16))
        args.append(w)
    return in_specs, out_specs, out_shapes, args


def _run_casts(src_refs, dst_refs):
    for src, dst in zip(src_refs, dst_refs):
        dst[...] = src[...].astype(dst.dtype)


def _mm_resid_kernel(*refs, n_a, n_c):
    a_refs, w_refs = refs[:n_a], refs[n_a:2 * n_a]
    resid_ref, gate_ref = refs[2 * n_a:2 * n_a + 2]
    cast_src = refs[2 * n_a + 2:2 * n_a + 2 + n_c]
    o_ref = refs[2 * n_a + 2 + n_c]
    cast_dst = refs[2 * n_a + 3 + n_c:]
    acc = _dot(a_refs[0][...], w_refs[0][...])
    for a_ref, w_ref in zip(a_refs[1:], w_refs[1:]):
        acc = acc + _dot(a_ref[...], w_ref[...])
    o_ref[...] = resid_ref[...] + gate_ref[...] * acc
    _run_casts(cast_src, cast_dst)


def matmul_resid(a_list, w, layer, resid, gate, tm, tn, casts=()):
    m, n = resid.shape
    nb = gate.shape[0]
    assert (m // tm) % nb == 0
    tpb = (m // tm) // nb
    n_a = len(a_list)
    c_in, c_out, c_shapes, c_args = _cast_jobs(casts, m // tm, n // tn)
    in_specs, w_args, off = [], [], 0
    for a in a_list:
        in_specs.append(pl.BlockSpec((tm, a.shape[1]), lambda i, j: (i, 0)))
    for a in a_list:
        ka = a.shape[1]
        assert off % ka == 0
        in_specs.append(pl.BlockSpec((None, ka, tn), lambda i, j, r=off // ka: (layer, r, j)))
        w_args.append(w)
        off += ka
    assert off == w.shape[1]
    in_specs += [pl.BlockSpec((tm, tn), lambda i, j: (i, j)),
                 pl.BlockSpec((None, 1, tn), lambda i, j: (i // tpb, 0, j))]
    outs = pl.pallas_call(
        functools.partial(_mm_resid_kernel, n_a=n_a, n_c=len(casts)),
        grid=(m // tm, n // tn),
        in_specs=in_specs + c_in,
        out_specs=[pl.BlockSpec((tm, tn), lambda i, j: (i, j))] + c_out,
        out_shape=[jax.ShapeDtypeStruct((m, n), F32)] + c_shapes,
        compiler_params=_cparams(("arbitrary", "arbitrary")),
        name="matmul_resid",
    )(*a_list, *w_args, resid, gate, *c_args)
    return outs[0], outs[1:]


def _mm_swiglu_kernel(a_ref, wg_ref, wu_ref, *refs, n_c):
    cast_src, o_ref, cast_dst = refs[:n_c], refs[n_c], refs[n_c + 1:]
    a = a_ref[...]
    g = _dot(a, wg_ref[...])
    u = _dot(a, wu_ref[...])
    o_ref[...] = (g * _sigmoid(g) * u).astype(o_ref.dtype)
    _run_casts(cast_src, cast_dst)


def matmul_swiglu(a, wg, wu, layer, tm, tn, casts=()):
    m, k = a.shape
    n = wg.shape[2]
    wspec = pl.BlockSpec((None, k, tn), lambda i, j: (layer, 0, j))
    c_in, c_out, c_shapes, c_args = _cast_jobs(casts, m // tm, n // tn)
    outs = pl.pallas_call(
        functools.partial(_mm_swiglu_kernel, n_c=len(casts)),
        grid=(m // tm, n // tn),
        in_specs=[pl.BlockSpec((tm, k), lambda i, j: (i, 0)), wspec, wspec] + c_in,
        out_specs=[pl.BlockSpec((tm, tn), lambda i, j: (i, j))] + c_out,
        out_shape=[jax.ShapeDtypeStruct((m, n), BF16)] + c_shapes,
        compiler_params=_cparams(("arbitrary", "arbitrary")),
        name="matmul_swiglu",
    )(a, wg, wu, *c_args)
    return outs[0], outs[1:]


def _rope_slab(x, cos, sin_signed, even):
    swapped = jnp.where(even, pltpu.roll(x, LANES - 1, 1), pltpu.roll(x, 1, 1))
    return x * cos + swapped * sin_signed


def _even_lanes(rows):
    return (lax.broadcasted_iota(jnp.int32, (rows, LANES), 1) % 2) == 0


def _sgu_kernel(u_ref, v_ref, gain_ref, w_ref, b_ref, o_ref):
    tm = u_ref.shape[0]
    groups = w_ref.shape[0]
    v = _gelu(v_ref[...].astype(F32))
    ms = jnp.mean(v * v, axis=-1, keepdims=True)
    vn = (v * lax.rsqrt(ms + EPS) * gain_ref[...]).astype(BF16)
    bias = b_ref[...]
    for c in range(tm // SGU_CHUNK):
        rows = slice(c * SGU_CHUNK, (c + 1) * SGU_CHUNK)
        for g in range(groups):
            cols = slice(g * SGU_GROUP, (g + 1) * SGU_GROUP)
            mixed = _dot(w_ref[g], vn[rows, cols]) + bias[:, g:g + 1]
            u = _gelu(u_ref[rows, cols].astype(F32))
            o_ref[rows, cols] = (u * mixed).astype(o_ref.dtype)


def sgu(p, gain, w_s, b_s, tm=256):
    m = p.shape[0]
    groups = w_s.shape[0]
    width = groups * SGU_GROUP
    return pl.pallas_call(
        _sgu_kernel,
        grid=(m // tm,),
        in_specs=[pl.BlockSpec((tm, width), lambda i: (i, 0)),
                  pl.BlockSpec((tm, width), lambda i: (i, 1)),
                  pl.BlockSpec((1, width), lambda i: (0, 0)),
                  pl.BlockSpec((groups, SGU_CHUNK, SGU_CHUNK), lambda i: (0, 0, 0)),
                  pl.BlockSpec((SGU_CHUNK, groups), lambda i: (0, 0))],
        out_specs=pl.BlockSpec((tm, width), lambda i: (i, 0)),
        out_shape=jax.ShapeDtypeStruct((m, width), BF16),
        compiler_params=_cparams(("parallel",)),
        name="sgu",
    )(p, p, gain.reshape(1, width), w_s.astype(BF16), b_s.T)


def _qkv_prep_kernel(q_ref, k_ref, v_ref, qg_ref, kg_ref, cos_ref, sin_ref,
                     qo_ref, ko_ref, vo_ref, *, rope, q_scale):
    tm = q_ref.shape[0]
    even = _even_lanes(tm)
    cos = cos_ref[...]
    sin = sin_ref[...]

    def prep(x, gain):
        ms = jnp.mean(x * x, axis=-1, keepdims=True)
        xn = x * lax.rsqrt(ms + EPS) * gain
        if rope:
            xn = _rope_slab(xn, cos, sin, even)
        return xn

    qg = qg_ref[...]
    kg = kg_ref[...]
    for h in range(q_ref.shape[1] // HEAD_DIM):
        cols = slice(h * HEAD_DIM, (h + 1) * HEAD_DIM)
        qo_ref[:, cols] = (prep(q_ref[:, cols].astype(F32), qg) * q_scale).astype(qo_ref.dtype)
    for h in range(k_ref.shape[1] // HEAD_DIM):
        cols = slice(h * HEAD_DIM, (h + 1) * HEAD_DIM)
        ko_ref[:, cols] = prep(k_ref[:, cols].astype(F32), kg).astype(ko_ref.dtype)
    vo_ref[...] = v_ref[...].astype(vo_ref.dtype)


def qkv_prep(p, q_gain, k_gain, cos, sin, col0, n_q, n_kv, rope, tm=256):
    m = p.shape[0]
    qw, kw = n_q * HEAD_DIM, n_kv * HEAD_DIM
    assert col0 % qw == 0 and (col0 + qw) % kw == 0
    pos_tiles = cos.shape[0] // tm
    tab = pl.BlockSpec((tm, HEAD_DIM), lambda i: (i % pos_tiles, 0))
    kern = functools.partial(_qkv_prep_kernel, rope=rope, q_scale=HEAD_DIM ** -0.5 * LOG2E)
    return pl.pallas_call(
        kern,
        grid=(m // tm,),
        in_specs=[pl.BlockSpec((tm, qw), lambda i: (i, col0 // qw)),
                  pl.BlockSpec((tm, kw), lambda i: (i, (col0 + qw) // kw)),
                  pl.BlockSpec((tm, kw), lambda i: (i, (col0 + qw) // kw + 1)),
                  pl.BlockSpec((1, HEAD_DIM), lambda i: (0, 0)),
                  pl.BlockSpec((1, HEAD_DIM), lambda i: (0, 0)),
                  tab, tab],
        out_specs=[pl.BlockSpec((tm, qw), lambda i: (i, 0)),
                   pl.BlockSpec((tm, kw), lambda i: (i, 0)),
                   pl.BlockSpec((tm, kw), lambda i: (i, 0))],
        out_shape=[jax.ShapeDtypeStruct((m, qw), BF16),
                   jax.ShapeDtypeStruct((m, kw), BF16),
                   jax.ShapeDtypeStruct((m, kw), BF16)],
        compiler_params=_cparams(("parallel",)),
        name="qkv_prep",
    )(p, p, p, q_gain.reshape(1, HEAD_DIM), k_gain.reshape(1, HEAD_DIM), cos, sin)


def _flash_kernel(q_ref, k_ref, v_ref, o_ref, sa_ref, sb_ref, pa_ref, pb_ref,
                  m_ref, l_ref, acc_ref, alpha_ref, *, tk, group):
    tq = q_ref.shape[0]
    nk = k_ref.shape[0] // tk
    ncb = tk // LANES
    q = jnp.concatenate(
        [q_ref[:, g * HEAD_DIM:(g + 1) * HEAD_DIM] for g in range(group)], axis=0)
    m_ref[...] = jnp.full(m_ref.shape, -jnp.inf, F32)
    l_ref[...] = jnp.zeros(l_ref.shape, F32)
    acc_ref[...] = jnp.zeros(acc_ref.shape, F32)
    s_bufs = (sa_ref, sb_ref)
    p_bufs = (pa_ref, pb_ref)

    def scores(j, s_ref):
        off = pl.multiple_of(j * tk, tk)
        s_ref[...] = _dot_nt(q, k_ref[pl.ds(off, tk), :])

    def pv(j, p_ref):
        off = pl.multiple_of(j * tk, tk)
        acc_ref[...] += _dot(p_ref[...], v_ref[pl.ds(off, tk), :])

    def softmax(s_ref, p_ref):
        for r0 in range(0, group * tq, FLASH_ROWS):
            rows = slice(r0, r0 + FLASH_ROWS)
            blocks = [s_ref[rows, cb * LANES:(cb + 1) * LANES] for cb in range(ncb)]
            mx = blocks[0]
            for blk in blocks[1:]:
                mx = jnp.maximum(mx, blk)
            m_prev = m_ref[rows, :]
            m_new = jnp.maximum(
                m_prev, jnp.broadcast_to(jnp.max(mx, axis=-1, keepdims=True), mx.shape))
            alpha = jnp.exp2(m_prev - m_new)
            psum = None
            for cb, blk in enumerate(blocks):
                pb = jnp.exp2(blk - m_new)
                psum = pb if psum is None else psum + pb
                p_ref[rows, cb * LANES:(cb + 1) * LANES] = pb.astype(BF16)
            l_ref[rows, :] = alpha * l_ref[rows, :] + psum
            alpha_ref[rows, :] = alpha
            m_ref[rows, :] = m_new

    def stage(j, parity, has_next):
        pv(j - 1, p_bufs[1 - parity])
        if has_next:
            scores(j + 1, s_bufs[1 - parity])
        softmax(s_bufs[parity], p_bufs[parity])
        acc_ref[...] = acc_ref[...] * alpha_ref[...]

    scores(0, sa_ref)
    if nk > 1:
        scores(1, sb_ref)
    softmax(sa_ref, pa_ref)

    def body(jj, carry):
        j = 2 * jj + 1
        stage(j, 1, True)
        stage(j + 1, 0, True)
        return carry

    trips = max(nk - 2, 0) // 2
    lax.fori_loop(0, trips, body, 0)
    for j in range(2 * trips + 1, nk):
        stage(j, j % 2, j + 1 < nk)
    pv(nk - 1, p_bufs[(nk - 1) % 2])

    out = acc_ref[...] / jnp.sum(l_ref[...], axis=-1, keepdims=True)
    for g in range(group):
        o_ref[:, g * HEAD_DIM:(g + 1) * HEAD_DIM] = out[g * tq:(g + 1) * tq].astype(o_ref.dtype)


def flash_attention(q, k, v, tq, tk):
    b, n, qw = q.shape
    s = k.shape[1]
    kvh = k.shape[2] // HEAD_DIM
    group = qw // HEAD_DIM // kvh
    gw = group * HEAD_DIM
    assert s % tk == 0 and n % tq == 0
    return pl.pallas_call(
        functools.partial(_flash_kernel, tk=tk, group=group),
        grid=(b, kvh, n // tq),
        in_specs=[pl.BlockSpec((None, tq, gw), lambda bb, h, i: (bb, i, h)),
                  pl.BlockSpec((None, s, HEAD_DIM), lambda bb, h, i: (bb, 0, h)),
                  pl.BlockSpec((None, s, HEAD_DIM), lambda bb, h, i: (bb, 0, h))],
        out_specs=pl.BlockSpec((None, tq, gw), lambda bb, h, i: (bb, i, h)),
        out_shape=jax.ShapeDtypeStruct((b, n, qw), BF16),
        scratch_shapes=[pltpu.VMEM((group * tq, tk), F32),
                        pltpu.VMEM((group * tq, tk), F32),
                        pltpu.VMEM((group * tq, tk), BF16),
                        pltpu.VMEM((group * tq, tk), BF16),
                        pltpu.VMEM((group * tq, LANES), F32),
                        pltpu.VMEM((group * tq, LANES), F32),
                        pltpu.VMEM((group * tq, HEAD_DIM), F32),
                        pltpu.VMEM((group * tq, HEAD_DIM), F32)],
        compiler_params=_cparams(("parallel", "parallel", "arbitrary")),
        name="flash_attention",
    )(q, k, v)


def _ret_prep_kernel(q_ref, k_ref, cos_ref, sin_ref, qo_ref, ko_ref, *, rope, k_scale):
    tm = q_ref.shape[0]
    even = _even_lanes(tm)
    slabs_per_head = RET_DIM // LANES
    for sl in range(q_ref.shape[1] // LANES):
        cols = slice(sl * LANES, (sl + 1) * LANES)
        tcols = slice((sl % slabs_per_head) * LANES, (sl % slabs_per_head + 1) * LANES)
        q = q_ref[:, cols].astype(F32)
        k = k_ref[:, cols].astype(F32) * k_scale
        if rope:
            cos = cos_ref[:, tcols]
            sin = sin_ref[:, tcols]
            q = _rope_slab(q, cos, sin, even)
            k = _rope_slab(k, cos, sin, even)
        qo_ref[:, cols] = q.astype(qo_ref.dtype)
        ko_ref[:, cols] = k.astype(ko_ref.dtype)


def ret_prep(p, cos, sin, width, rope, tm=256):
    m = p.shape[0]
    pos_tiles = cos.shape[0] // tm
    tab = pl.BlockSpec((tm, RET_DIM), lambda i: (i % pos_tiles, 0))
    out = jax.ShapeDtypeStruct((m, width), BF16)
    ospec = pl.BlockSpec((tm, width), lambda i: (i, 0))
    return pl.pallas_call(
        functools.partial(_ret_prep_kernel, rope=rope, k_scale=RET_DIM ** -0.5),
        grid=(m // tm,),
        in_specs=[pl.BlockSpec((tm, width), lambda i: (i, 0)),
                  pl.BlockSpec((tm, width), lambda i: (i, 1)),
                  tab, tab],
        out_specs=[ospec, ospec],
        out_shape=[out, out],
        compiler_params=_cparams(("parallel",)),
        name="ret_prep",
    )(p, p, cos, sin)


def _ret_kernel(dec_ref, qf_ref, kf_ref, vf_ref, qb_ref, kb_ref, vb_ref, s0_ref,
                of_ref, ob_ref, sfin_ref, st_ref, dm_ref, rd_ref, *, hb):
    c = RET_CHUNK
    s = pl.program_id(2)

    @pl.when(s == 0)
    def _():
        st_ref[...] = s0_ref[...]
        row = lax.broadcasted_iota(jnp.int32, (c, c), 0).astype(F32)
        col = lax.broadcasted_iota(jnp.int32, (c, c), 1).astype(F32)
        for d in range(2):
            for hh in range(hb):
                ld = -jnp.exp(dec_ref[d, hh])
                if d == 0:
                    dist, keep = row - col, row >= col
                    q_pow, k_pow = row + 1.0, (c - 1.0) - row
                else:
                    dist, keep = col - row, col > row
                    q_pow, k_pow = c - row, row
                dm_ref[d, hh] = jnp.where(keep, jnp.exp(ld * jnp.maximum(dist, 0.0)), 0.0)
                rd_ref[d, hh, 0] = jnp.exp(ld * q_pow)
                rd_ref[d, hh, 1] = jnp.exp(ld * k_pow)

    streams = ((qf_ref, kf_ref, vf_ref, of_ref), (qb_ref, kb_ref, vb_ref, ob_ref))
    for d, (q_ref, k_ref, v_ref, o_ref) in enumerate(streams):
        for hh in range(hb):
            cols = slice(hh * RET_DIM, (hh + 1) * RET_DIM)
            q = q_ref[:, cols]
            k = k_ref[:, cols]
            v = v_ref[:, cols]
            state = st_ref[d, hh]
            att = _dot_nt(q, k) * dm_ref[d, hh]
            o = _dot(att.astype(BF16), v) + rd_ref[d, hh, 0] * _dot(q, state.astype(BF16))
            kt = (k.astype(F32) * rd_ref[d, hh, 1]).astype(BF16)
            chunk_decay = jnp.exp(-jnp.exp(dec_ref[d, hh]) * float(c))
            st_ref[d, hh] = state * chunk_decay + _dot_tn(kt, v)
            o_ref[:, cols] = o

    @pl.when(s == pl.num_programs(2) - 1)
    def _():
        sfin_ref[...] = st_ref[...]


def retention(q, k, p, dec, s0, hb=4):
    b, n, w = q.shape
    h = w // RET_DIM
    c = RET_CHUNK
    nc = n // c
    v0 = 2 * (h // hb)
    fw = pl.BlockSpec((None, c, hb * RET_DIM), lambda bb, g, s: (bb, s, g))
    bw = pl.BlockSpec((None, c, hb * RET_DIM), lambda bb, g, s: (bb, nc - 1 - s, g))
    vfw = pl.BlockSpec((None, c, hb * RET_DIM), lambda bb, g, s: (bb, s, v0 + g))
    vbw = pl.BlockSpec((None, c, hb * RET_DIM), lambda bb, g, s: (bb, nc - 1 - s, v0 + g))
    st = pl.BlockSpec((None, 2, hb, RET_DIM, RET_DIM), lambda bb, g, s: (bb, 0, g, 0, 0))
    return pl.pallas_call(
        functools.partial(_ret_kernel, hb=hb),
        grid=(b, h // hb, nc),
        in_specs=[pl.BlockSpec((2, hb, 1, RET_DIM), lambda bb, g, s: (0, g, 0, 0)),
                  fw, fw, vfw, bw, bw, vbw, st],
        out_specs=[fw, bw, st],
        out_shape=[jax.ShapeDtypeStruct((b, n, w), F32),
                   jax.ShapeDtypeStruct((b, n, w), F32),
                   jax.ShapeDtypeStruct(s0.shape, F32)],
        scratch_shapes=[pltpu.VMEM((2, hb, RET_DIM, RET_DIM), F32),
                        pltpu.VMEM((2, hb, c, c), F32),
                        pltpu.VMEM((2, hb, 2, c, RET_DIM), F32)],
        compiler_params=_cparams(("parallel", "parallel", "arbitrary")),
        name="retention",
    )(dec, q, k, p, q, k, p, s0)


def _ret_out_kernel(of_ref, ob_ref, g_ref, o_ref):
    for h in range(of_ref.shape[1] // RET_DIM):
        cols = slice(h * RET_DIM, (h + 1) * RET_DIM)
        o = of_ref[:, cols] + ob_ref[:, cols]
        ms = jnp.mean(o * o, axis=-1, keepdims=True)
        g = g_ref[:, cols].astype(F32)
        o_ref[:, cols] = (g * _sigmoid(g) * (o * lax.rsqrt(ms + EPS))).astype(o_ref.dtype)


def ret_out(o_f, o_b, p, tm=256):
    m, w = o_f.shape
    spec = pl.BlockSpec((tm, w), lambda i: (i, 0))
    return pl.pallas_call(
        _ret_out_kernel,
        grid=(m // tm,),
        in_specs=[spec, spec, pl.BlockSpec((tm, w), lambda i: (i, 3))],
        out_specs=spec,
        out_shape=jax.ShapeDtypeStruct((m, w), BF16),
        compiler_params=_cparams(("parallel",)),
        name="ret_out",
    )(o_f, o_b, p)


def _rope_tables(rows, head_dim):
    row = jnp.repeat(jnp.arange(rows), GRID_W).astype(F32)
    col = jnp.tile(jnp.arange(GRID_W), rows).astype(F32)
    n_freq = head_dim // 4
    inv_freq = ROPE_THETA ** (-jnp.arange(n_freq, dtype=F32) / n_freq)
    ang = jnp.concatenate([row[:, None] * inv_freq, col[:, None] * inv_freq], axis=-1)
    sign = jnp.tile(jnp.array([-1.0, 1.0], F32), head_dim // 2)
    return (jnp.repeat(jnp.cos(ang), 2, axis=-1),
            jnp.repeat(jnp.sin(ang), 2, axis=-1) * sign)


def _row_tile(m, pref):
    return pref if m % pref == 0 else m


def _ffn(h, mods, norm_gain, wg, wu, wd, gate_up_casts=(), down_casts=()):
    sh2, sc2, g2 = mods
    m = h.shape[0]
    hn = norm_mod(h, norm_gain, sc2, sh2)
    act, cast1 = matmul_swiglu(hn, wg, wu, 0, tm=_row_tile(m, 2048), tn=256,
                               casts=gate_up_casts)
    out, cast2 = matmul_resid([act], wd, 0, h, g2, tm=512, tn=512, casts=down_casts)
    return out, cast1, cast2


def kernel(x, c, ctx, c_ctx, norm_mix, norm_ffn, mod_down, mod_up, mod_bias,
           ffn_gate, ffn_up, ffn_down, ev_w_in, ev_w_out, sgu_norm, sgu_w, sgu_b,
           q_norm, k_norm, ret_w_in, ret_w_out, ret_decay_fwd, ret_decay_bwd):
    b, n, d = x.shape
    n_ctx = ctx.shape[1]
    depth = norm_mix.shape[0]
    a_width = sgu_w.shape[1] * SGU_GROUP
    n_q = (ev_w_out.shape[1] - a_width) // HEAD_DIM
    ret_w = ret_w_out.shape[1]
    ret_heads = ret_w // RET_DIM

    cv = jnp.concatenate([c, c_ctx[None], jnp.zeros((8 - b - 1, d), F32)], axis=0)
    mods = adaln_all(cv, mod_down, mod_up, mod_bias)

    cos_a, sin_a = _rope_tables(n // GRID_W, HEAD_DIM)
    cos_r, sin_r = _rope_tables(n // GRID_W, RET_DIM)

    def first(w):
        return w[:1].astype(BF16)

    mix_in, mix_out = first(ev_w_in), first(ev_w_out)
    wg = wu = wd = None

    h_lat = x.reshape(b * n, d)
    h_ctx = ctx.reshape(b * n_ctx, d)
    for i in range(depth):
        last = i == depth - 1
        j = i // 2
        lat_mod = [mods[i, :b, t * d:(t + 1) * d].reshape(b, 1, d) for t in range(N_MOD)]
        ctx_mod = [mods[i, b:b + 1, t * d:(t + 1) * d].reshape(1, 1, d) for t in range(N_MOD)]
        xn = norm_mod(h_lat, norm_mix[i], lat_mod[1], lat_mod[0])
        cn = norm_mod(h_ctx, norm_mix[i], ctx_mod[1], ctx_mod[0])
        if i % 2 == 0:
            in_casts = [(ffn_gate, 0), (ffn_up, 0)] if i == 0 else []
            out_casts = [(ffn_down, 0)] if i == 0 else []
            p_l, cast_in = matmul(xn, mix_in, 0, BF16, tm=1024, tn=512, casts=in_casts)
            p_c, _ = matmul(cn, mix_in, 0, BF16, tm=b * n_ctx, tn=512)
            ql, kl, vl = qkv_prep(p_l, q_norm[j], k_norm[j], cos_a, sin_a,
                                  2 * a_width, n_q, KV_HEADS, rope=True)
            qc, kc, vc = qkv_prep(p_c, q_norm[j], k_norm[j], cos_a, sin_a,
                                  2 * a_width, n_q, KV_HEADS, rope=False)
            kw = KV_HEADS * HEAD_DIM
            k_all = jnp.concatenate([kc.reshape(b, n_ctx, kw), kl.reshape(b, n, kw)], axis=1)
            v_all = jnp.concatenate([vc.reshape(b, n_ctx, kw), vl.reshape(b, n, kw)], axis=1)
            tk = next(t for t in (384, 512, 256) if (n_ctx + n) % t == 0)
            att_l = flash_attention(ql.reshape(b, n, -1), k_all, v_all, tq=256, tk=tk)
            sgu_l = sgu(p_l, sgu_norm[j], sgu_w[j], sgu_b[j])
            h_lat, cast_out = matmul_resid([sgu_l, att_l.reshape(b * n, -1)], mix_out, 0, h_lat,
                                           lat_mod[2], tm=1024, tn=512, casts=out_casts)
            if i == 0:
                (wg, wu), (wd,) = cast_in, cast_out
            if not last:
                att_c = flash_attention(qc.reshape(b, n_ctx, -1), kc.reshape(b, n_ctx, kw),
                                        vc.reshape(b, n_ctx, kw), tq=n_ctx, tk=n_ctx)
                sgu_c = sgu(p_c, sgu_norm[j], sgu_w[j], sgu_b[j])
                h_ctx, _ = matmul_resid([sgu_c, att_c.reshape(b * n_ctx, -1)], mix_out, 0,
                                        h_ctx, ctx_mod[2], tm=b * n_ctx, tn=512)
        else:
            p_l, _ = matmul(xn, mix_in, 0, BF16, tm=1024, tn=512)
            p_c, _ = matmul(cn, mix_in, 0, BF16, tm=b * n_ctx, tn=512)
            ql, kl = ret_prep(p_l, cos_r, sin_r, ret_w, rope=True)
            qc, kc = ret_prep(p_c, cos_r, sin_r, ret_w, rope=False)
            dec = jnp.stack([ret_decay_fwd[j], ret_decay_bwd[j]]).astype(F32)
            dec = jnp.broadcast_to(dec[:, :, None, None], (2, ret_heads, 1, RET_DIM))
            s0 = jnp.zeros((b, 2, ret_heads, RET_DIM, RET_DIM), F32)
            shp_c, shp_l = (b, n_ctx, ret_w), (b, n, ret_w)
            ocf, ocb, s_ctx = retention(qc.reshape(shp_c), kc.reshape(shp_c),
                                        p_c.reshape(b, n_ctx, -1), dec, s0)
            olf, olb, _ = retention(ql.reshape(shp_l), kl.reshape(shp_l),
                                    p_l.reshape(b, n, -1), dec, s_ctx)
            y_l = ret_out(olf.reshape(b * n, ret_w), olb.reshape(b * n, ret_w), p_l)
            h_lat, _ = matmul_resid([y_l], mix_out, 0, h_lat, lat_mod[2], tm=1024, tn=512)
            if not last:
                y_c = ret_out(ocf.reshape(b * n_ctx, ret_w), ocb.reshape(b * n_ctx, ret_w), p_c)
                h_ctx, _ = matmul_resid([y_c], mix_out, 0, h_ctx, ctx_mod[2],
                                        tm=b * n_ctx, tn=512)

        if last:
            h_lat, _, _ = _ffn(h_lat, lat_mod[3:], norm_ffn[i], wg, wu, wd)
        else:
            nj = (i + 1) // 2
            nxt_in, nxt_out = (ev_w_in, ev_w_out) if (i + 1) % 2 == 0 else (ret_w_in, ret_w_out)
            h_lat, cast1, cast2 = _ffn(
                h_lat, lat_mod[3:], norm_ffn[i], wg, wu, wd,
                gate_up_casts=[(ffn_down, i + 1), (ffn_gate, i + 1), (nxt_in, nj)],
                down_casts=[(ffn_up, i + 1), (nxt_out, nj)])
            h_ctx, _, _ = _ffn(h_ctx, ctx_mod[3:], norm_ffn[i], wg, wu, wd)
            (wd, wg, mix_in), (wu, mix_out) = cast1, cast2
    return h_lat.reshape(b, n, d)
```

```python
import functools

import jax
import jax.numpy as jnp
from jax import lax
from jax.experimental import pallas as pl
from jax.experimental.pallas import tpu as pltpu

F32 = jnp.float32
BF16 = jnp.bfloat16

EPS = 1e-6
GRID_W = 64
N_MOD = 6
ROPE_THETA = 10000.0
SGU_CHUNK = 128
SGU_GROUP = 128
HEAD_DIM = 128
KV_HEADS = 4
RET_DIM = 256
RET_CHUNK = 256
LANES = 128
BF16_SUBLANES = 16
LOG2E = 1.4426950408889634
FLASH_UNROLL = 4
FLASH_ROWS = 64
VMEM_LIMIT = 56 * 1024 * 1024


def _cparams(sem):
    return pltpu.CompilerParams(dimension_semantics=sem, vmem_limit_bytes=VMEM_LIMIT)


def _sigmoid(x):
    return 1.0 / (1.0 + jnp.exp(-x))


def _gelu(x):
    return 0.5 * x * (1.0 + jnp.tanh(0.7978845608028654 * (x + 0.044715 * (x * x * x))))


def _dot(a, b):
    return jnp.dot(a, b, preferred_element_type=F32)


def _dot_nt(a, b):
    return lax.dot_general(a, b, (((1,), (1,)), ((), ())), preferred_element_type=F32)


def _dot_tn(a, b):
    return lax.dot_general(a, b, (((0,), (0,)), ((), ())), preferred_element_type=F32)


def _adaln_kernel(cv_ref, down_ref, up_ref, bias_ref, o_ref, hd_ref):
    @pl.when(pl.program_id(1) == 0)
    def _():
        cv = cv_ref[...]
        hd_ref[...] = _dot(cv * _sigmoid(cv), down_ref[...])

    o_ref[...] = _dot(hd_ref[...], up_ref[...]) + bias_ref[...]


def adaln_all(cv, down, up, bias, tn=2048):
    depth, d, rank = down.shape
    nout = up.shape[-1]
    rows = cv.shape[0]
    return pl.pallas_call(
        _adaln_kernel,
        grid=(depth, nout // tn),
        in_specs=[
            pl.BlockSpec((rows, d), lambda l, j: (0, 0)),
            pl.BlockSpec((None, d, rank), lambda l, j: (l, 0, 0)),
            pl.BlockSpec((None, rank, tn), lambda l, j: (l, 0, j)),
            pl.BlockSpec((None, 1, tn), lambda l, j: (l, 0, j)),
        ],
        out_specs=pl.BlockSpec((None, rows, tn), lambda l, j: (l, 0, j)),
        out_shape=jax.ShapeDtypeStruct((depth, rows, nout), F32),
        scratch_shapes=[pltpu.VMEM((rows, rank), F32)],
        compiler_params=_cparams(("arbitrary", "arbitrary")),
        name="adaln",
    )(cv, down, up, bias.reshape(depth, 1, nout))


def _norm_mod_kernel(x_ref, gain_ref, scale_ref, shift_ref, o_ref):
    x = x_ref[...]
    ms = jnp.mean(x * x, axis=-1, keepdims=True)
    xn = x * lax.rsqrt(ms + EPS) * gain_ref[...]
    o_ref[...] = (xn * (1.0 + scale_ref[...]) + shift_ref[...]).astype(o_ref.dtype)


def norm_mod(x, gain, scale, shift, tm=512):
    m, d = x.shape
    nb = scale.shape[0]
    assert (m // tm) % nb == 0
    tpb = (m // tm) // nb
    vec = pl.BlockSpec((None, 1, d), lambda i: (i // tpb, 0, 0))
    return pl.pallas_call(
        _norm_mod_kernel,
        grid=(m // tm,),
        in_specs=[pl.BlockSpec((tm, d), lambda i: (i, 0)),
                  pl.BlockSpec((1, d), lambda i: (0, 0)), vec, vec],
        out_specs=pl.BlockSpec((tm, d), lambda i: (i, 0)),
        out_shape=jax.ShapeDtypeStruct((m, d), BF16),
        compiler_params=_cparams(("parallel",)),
        name="norm_mod",
    )(x, gain.reshape(1, d), scale, shift)


def _mm_store_kernel(a_ref, w_ref, *refs, n_c):
    cast_src, o_ref, cast_dst = refs[:n_c], refs[n_c], refs[n_c + 1:]
    o_ref[...] = _dot(a_ref[...], w_ref[...]).astype(o_ref.dtype)
    _run_casts(cast_src, cast_dst)


def matmul(a, w, layer, out_dtype, tm, tn, casts=()):
    m, k = a.shape
    n = w.shape[2]
    c_in, c_out, c_shapes, c_args = _cast_jobs(casts, m // tm, n // tn)
    outs = pl.pallas_call(
        functools.partial(_mm_store_kernel, n_c=len(casts)),
        grid=(m // tm, n // tn),
        in_specs=[pl.BlockSpec((tm, k), lambda i, j: (i, 0)),
                  pl.BlockSpec((None, k, tn), lambda i, j: (layer, 0, j))] + c_in,
        out_specs=[pl.BlockSpec((tm, tn), lambda i, j: (i, j))] + c_out,
        out_shape=[jax.ShapeDtypeStruct((m, n), out_dtype)] + c_shapes,
        compiler_params=_cparams(("arbitrary", "arbitrary")),
        name="matmul",
    )(a, w, *c_args)
    return outs[0], outs[1:]


def _cast_jobs(casts, steps_i, steps_j):
    in_specs, out_specs, out_shapes, args = [], [], [], []
    for w, layer in casts:
        _, k, c = w.shape
        rows = next(r for r in range(BF16_SUBLANES, k + 1, BF16_SUBLANES)
                    if k % r == 0 and k // r <= steps_i * steps_j)
        last = k // rows - 1
        in_specs.append(pl.BlockSpec(
            (None, rows, c),
            lambda i, j, l=layer, e=last: (l, jnp.minimum(i * steps_j + j, e), 0)))
        out_specs.append(pl.BlockSpec(
            (None, rows, c), lambda i, j, e=last: (0, jnp.minimum(i * steps_j + j, e), 0)))
        out_shapes.append(jax.ShapeDtypeStruct((1, k, c), BF16))
        args.append(w)
    return in_specs, out_specs, out_shapes, args


def _run_casts(src_refs, dst_refs):
    for src, dst in zip(src_refs, dst_refs):
        dst[...] = src[...].astype(dst.dtype)


def _mm_resid_kernel(*refs, n_a, n_c):
    a_refs, w_refs = refs[:n_a], refs[n_a:2 * n_a]
    resid_ref, gate_ref = refs[2 * n_a:2 * n_a + 2]
    cast_src = refs[2 * n_a + 2:2 * n_a + 2 + n_c]
    o_ref = refs[2 * n_a + 2 + n_c]
    cast_dst = refs[2 * n_a + 3 + n_c:]
    acc = _dot(a_refs[0][...], w_refs[0][...])
    for a_ref, w_ref in zip(a_refs[1:], w_refs[1:]):
        acc = acc + _dot(a_ref[...], w_ref[...])
    o_ref[...] = resid_ref[...] + gate_ref[...] * acc
    _run_casts(cast_src, cast_dst)


def matmul_resid(a_list, w, layer, resid, gate, tm, tn, casts=()):
    m, n = resid.shape
    nb = gate.shape[0]
    assert (m // tm) % nb == 0
    tpb = (m // tm) // nb
    n_a = len(a_list)
    c_in, c_out, c_shapes, c_args = _cast_jobs(casts, m // tm, n // tn)
    in_specs, w_args, off = [], [], 0
    for a in a_list:
        in_specs.append(pl.BlockSpec((tm, a.shape[1]), lambda i, j: (i, 0)))
    for a in a_list:
        ka = a.shape[1]
        assert off % ka == 0
        in_specs.append(pl.BlockSpec((None, ka, tn), lambda i, j, r=off // ka: (layer, r, j)))
        w_args.append(w)
        off += ka
    assert off == w.shape[1]
    in_specs += [pl.BlockSpec((tm, tn), lambda i, j: (i, j)),
                 pl.BlockSpec((None, 1, tn), lambda i, j: (i // tpb, 0, j))]
    outs = pl.pallas_call(
        functools.partial(_mm_resid_kernel, n_a=n_a, n_c=len(casts)),
        grid=(m // tm, n // tn),
        in_specs=in_specs + c_in,
        out_specs=[pl.BlockSpec((tm, tn), lambda i, j: (i, j))] + c_out,
        out_shape=[jax.ShapeDtypeStruct((m, n), F32)] + c_shapes,
        compiler_params=_cparams(("arbitrary", "arbitrary")),
        name="matmul_resid",
    )(*a_list, *w_args, resid, gate, *c_args)
    return outs[0], outs[1:]


def _mm_swiglu_kernel(a_ref, wg_ref, wu_ref, *refs, n_c):
    cast_src, o_ref, cast_dst = refs[:n_c], refs[n_c], refs[n_c + 1:]
    a = a_ref[...]
    g = _dot(a, wg_ref[...])
    u = _dot(a, wu_ref[...])
    o_ref[...] = (g * _sigmoid(g) * u).astype(o_ref.dtype)
    _run_casts(cast_src, cast_dst)


def matmul_swiglu(a, wg, wu, layer, tm, tn, casts=()):
    m, k = a.shape
    n = wg.shape[2]
    wspec = pl.BlockSpec((None, k, tn), lambda i, j: (layer, 0, j))
    c_in, c_out, c_shapes, c_args = _cast_jobs(casts, m // tm, n // tn)
    outs = pl.pallas_call(
        functools.partial(_mm_swiglu_kernel, n_c=len(casts)),
        grid=(m // tm, n // tn),
        in_specs=[pl.BlockSpec((tm, k), lambda i, j: (i, 0)), wspec, wspec] + c_in,
        out_specs=[pl.BlockSpec((tm, tn), lambda i, j: (i, j))] + c_out,
        out_shape=[jax.ShapeDtypeStruct((m, n), BF16)] + c_shapes,
        compiler_params=_cparams(("arbitrary", "arbitrary")),
        name="matmul_swiglu",
    )(a, wg, wu, *c_args)
    return outs[0], outs[1:]


def _rope_slab(x, cos, sin_signed, even):
    swapped = jnp.where(even, pltpu.roll(x, LANES - 1, 1), pltpu.roll(x, 1, 1))
    return x * cos + swapped * sin_signed


def _even_lanes(rows):
    return (lax.broadcasted_iota(jnp.int32, (rows, LANES), 1) % 2) == 0


def _sgu_kernel(u_ref, v_ref, gain_ref, w_ref, b_ref, o_ref):
    tm = u_ref.shape[0]
    groups = w_ref.shape[0]
    v = _gelu(v_ref[...].astype(F32))
    ms = jnp.mean(v * v, axis=-1, keepdims=True)
    vn = (v * lax.rsqrt(ms + EPS) * gain_ref[...]).astype(BF16)
    bias = b_ref[...]
    for c in range(tm // SGU_CHUNK):
        rows = slice(c * SGU_CHUNK, (c + 1) * SGU_CHUNK)
        for g in range(groups):
            cols = slice(g * SGU_GROUP, (g + 1) * SGU_GROUP)
            mixed = _dot(w_ref[g], vn[rows, cols]) + bias[:, g:g + 1]
            u = _gelu(u_ref[rows, cols].astype(F32))
            o_ref[rows, cols] = (u * mixed).astype(o_ref.dtype)


def sgu(p, gain, w_s, b_s, tm=256):
    m = p.shape[0]
    groups = w_s.shape[0]
    width = groups * SGU_GROUP
    return pl.pallas_call(
        _sgu_kernel,
        grid=(m // tm,),
        in_specs=[pl.BlockSpec((tm, width), lambda i: (i, 0)),
                  pl.BlockSpec((tm, width), lambda i: (i, 1)),
                  pl.BlockSpec((1, width), lambda i: (0, 0)),
                  pl.BlockSpec((groups, SGU_CHUNK, SGU_CHUNK), lambda i: (0, 0, 0)),
                  pl.BlockSpec((SGU_CHUNK, groups), lambda i: (0, 0))],
        out_specs=pl.BlockSpec((tm, width), lambda i: (i, 0)),
        out_shape=jax.ShapeDtypeStruct((m, width), BF16),
        compiler_params=_cparams(("parallel",)),
        name="sgu",
    )(p, p, gain.reshape(1, width), w_s.astype(BF16), b_s.T)


def _qkv_prep_kernel(q_ref, k_ref, v_ref, qg_ref, kg_ref, cos_ref, sin_ref,
                     qo_ref, ko_ref, vo_ref, *, rope, q_scale):
    tm = q_ref.shape[0]
    even = _even_lanes(tm)
    cos = cos_ref[...]
    sin = sin_ref[...]

    def prep(x, gain):
        ms = jnp.mean(x * x, axis=-1, keepdims=True)
        xn = x * lax.rsqrt(ms + EPS) * gain
        if rope:
            xn = _rope_slab(xn, cos, sin, even)
        return xn

    qg = qg_ref[...]
    kg = kg_ref[...]
    for h in range(q_ref.shape[1] // HEAD_DIM):
        cols = slice(h * HEAD_DIM, (h + 1) * HEAD_DIM)
        qo_ref[:, cols] = (prep(q_ref[:, cols].astype(F32), qg) * q_scale).astype(qo_ref.dtype)
    for h in range(k_ref.shape[1] // HEAD_DIM):
        cols = slice(h * HEAD_DIM, (h + 1) * HEAD_DIM)
        ko_ref[:, cols] = prep(k_ref[:, cols].astype(F32), kg).astype(ko_ref.dtype)
    vo_ref[...] = v_ref[...].astype(vo_ref.dtype)


def qkv_prep(p, q_gain, k_gain, cos, sin, col0, n_q, n_kv, rope, tm=256):
    m = p.shape[0]
    qw, kw = n_q * HEAD_DIM, n_kv * HEAD_DIM
    assert col0 % qw == 0 and (col0 + qw) % kw == 0
    pos_tiles = cos.shape[0] // tm
    tab = pl.BlockSpec((tm, HEAD_DIM), lambda i: (i % pos_tiles, 0))
    kern = functools.partial(_qkv_prep_kernel, rope=rope, q_scale=HEAD_DIM ** -0.5 * LOG2E)
    return pl.pallas_call(
        kern,
        grid=(m // tm,),
        in_specs=[pl.BlockSpec((tm, qw), lambda i: (i, col0 // qw)),
                  pl.BlockSpec((tm, kw), lambda i: (i, (col0 + qw) // kw)),
                  pl.BlockSpec((tm, kw), lambda i: (i, (col0 + qw) // kw + 1)),
                  pl.BlockSpec((1, HEAD_DIM), lambda i: (0, 0)),
                  pl.BlockSpec((1, HEAD_DIM), lambda i: (0, 0)),
                  tab, tab],
        out_specs=[pl.BlockSpec((tm, qw), lambda i: (i, 0)),
                   pl.BlockSpec((tm, kw), lambda i: (i, 0)),
                   pl.BlockSpec((tm, kw), lambda i: (i, 0))],
        out_shape=[jax.ShapeDtypeStruct((m, qw), BF16),
                   jax.ShapeDtypeStruct((m, kw), BF16),
                   jax.ShapeDtypeStruct((m, kw), BF16)],
        compiler_params=_cparams(("parallel",)),
        name="qkv_prep",
    )(p, p, p, q_gain.reshape(1, HEAD_DIM), k_gain.reshape(1, HEAD_DIM), cos, sin)


def _flash_kernel(q_ref, k_ref, v_ref, o_ref, sa_ref, sb_ref, pa_ref, pb_ref,
                  m_ref, l_ref, acc_ref, alpha_ref, *, tk, group):
    tq = q_ref.shape[0]
    nk = k_ref.shape[0] // tk
    ncb = tk // LANES
    q = jnp.concatenate(
        [q_ref[:, g * HEAD_DIM:(g + 1) * HEAD_DIM] for g in range(group)], axis=0)
    m_ref[...] = jnp.full(m_ref.shape, -jnp.inf, F32)
    l_ref[...] = jnp.zeros(l_ref.shape, F32)
    acc_ref[...] = jnp.zeros(acc_ref.shape, F32)
    s_bufs = (sa_ref, sb_ref)
    p_bufs = (pa_ref, pb_ref)

    def scores(j, s_ref):
        off = pl.multiple_of(j * tk, tk)
        s_ref[...] = _dot_nt(q, k_ref[pl.ds(off, tk), :])

    def pv(j, p_ref):
        off = pl.multiple_of(j * tk, tk)
        acc_ref[...] += _dot(p_ref[...], v_ref[pl.ds(off, tk), :])

    def softmax(s_ref, p_ref):
        for r0 in range(0, group * tq, FLASH_ROWS):
            rows = slice(r0, r0 + FLASH_ROWS)
            blocks = [s_ref[rows, cb * LANES:(cb + 1) * LANES] for cb in range(ncb)]
            mx = blocks[0]
            for blk in blocks[1:]:
                mx = jnp.maximum(mx, blk)
            m_prev = m_ref[rows, :]
            m_new = jnp.maximum(
                m_prev, jnp.broadcast_to(jnp.max(mx, axis=-1, keepdims=True), mx.shape))
            alpha = jnp.exp2(m_prev - m_new)
            psum = None
            for cb, blk in enumerate(blocks):
                pb = jnp.exp2(blk - m_new)
                psum = pb if psum is None else psum + pb
                p_ref[rows, cb * LANES:(cb + 1) * LANES] = pb.astype(BF16)
            l_ref[rows, :] = alpha * l_ref[rows, :] + psum
            alpha_ref[rows, :] = alpha
            m_ref[rows, :] = m_new

    def stage(j, parity, has_next):
        pv(j - 1, p_bufs[1 - parity])
        if has_next:
            scores(j + 1, s_bufs[1 - parity])
        softmax(s_bufs[parity], p_bufs[parity])
        acc_ref[...] = acc_ref[...] * alpha_ref[...]

    scores(0, sa_ref)
    if nk > 1:
        scores(1, sb_ref)
    softmax(sa_ref, pa_ref)

    def body(jj, carry):
        j = FLASH_UNROLL * jj + 1
        for t in range(FLASH_UNROLL):
            stage(j + t, (1 + t) % 2, True)
        return carry

    trips = max(nk - 2, 0) // FLASH_UNROLL
    lax.fori_loop(0, trips, body, 0)
    for j in range(FLASH_UNROLL * trips + 1, nk):
        stage(j, j % 2, j + 1 < nk)
    pv(nk - 1, p_bufs[(nk - 1) % 2])

    out = acc_ref[...] / jnp.sum(l_ref[...], axis=-1, keepdims=True)
    for g in range(group):
        o_ref[:, g * HEAD_DIM:(g + 1) * HEAD_DIM] = out[g * tq:(g + 1) * tq].astype(o_ref.dtype)


def flash_attention(q, k, v, tq, tk):
    b, n, qw = q.shape
    s = k.shape[1]
    kvh = k.shape[2] // HEAD_DIM
    group = qw // HEAD_DIM // kvh
    gw = group * HEAD_DIM
    assert s % tk == 0 and n % tq == 0
    return pl.pallas_call(
        functools.partial(_flash_kernel, tk=tk, group=group),
        grid=(b, kvh, n // tq),
        in_specs=[pl.BlockSpec((None, tq, gw), lambda bb, h, i: (bb, i, h)),
                  pl.BlockSpec((None, s, HEAD_DIM), lambda bb, h, i: (bb, 0, h)),
                  pl.BlockSpec((None, s, HEAD_DIM), lambda bb, h, i: (bb, 0, h))],
        out_specs=pl.BlockSpec((None, tq, gw), lambda bb, h, i: (bb, i, h)),
        out_shape=jax.ShapeDtypeStruct((b, n, qw), BF16),
        scratch_shapes=[pltpu.VMEM((group * tq, tk), F32),
                        pltpu.VMEM((group * tq, tk), F32),
                        pltpu.VMEM((group * tq, tk), BF16),
                        pltpu.VMEM((group * tq, tk), BF16),
                        pltpu.VMEM((group * tq, LANES), F32),
                        pltpu.VMEM((group * tq, LANES), F32),
                        pltpu.VMEM((group * tq, HEAD_DIM), F32),
                        pltpu.VMEM((group * tq, HEAD_DIM), F32)],
        compiler_params=_cparams(("parallel", "parallel", "arbitrary")),
        name="flash_attention",
    )(q, k, v)


def _ret_prep_kernel(q_ref, k_ref, cos_ref, sin_ref, qo_ref, ko_ref, *, rope, k_scale):
    tm = q_ref.shape[0]
    even = _even_lanes(tm)
    slabs_per_head = RET_DIM // LANES
    for sl in range(q_ref.shape[1] // LANES):
        cols = slice(sl * LANES, (sl + 1) * LANES)
        tcols = slice((sl % slabs_per_head) * LANES, (sl % slabs_per_head + 1) * LANES)
        q = q_ref[:, cols].astype(F32)
        k = k_ref[:, cols].astype(F32) * k_scale
        if rope:
            cos = cos_ref[:, tcols]
            sin = sin_ref[:, tcols]
            q = _rope_slab(q, cos, sin, even)
            k = _rope_slab(k, cos, sin, even)
        qo_ref[:, cols] = q.astype(qo_ref.dtype)
        ko_ref[:, cols] = k.astype(ko_ref.dtype)


def ret_prep(p, cos, sin, width, rope, tm=256):
    m = p.shape[0]
    pos_tiles = cos.shape[0] // tm
    tab = pl.BlockSpec((tm, RET_DIM), lambda i: (i % pos_tiles, 0))
    out = jax.ShapeDtypeStruct((m, width), BF16)
    ospec = pl.BlockSpec((tm, width), lambda i: (i, 0))
    return pl.pallas_call(
        functools.partial(_ret_prep_kernel, rope=rope, k_scale=RET_DIM ** -0.5),
        grid=(m // tm,),
        in_specs=[pl.BlockSpec((tm, width), lambda i: (i, 0)),
                  pl.BlockSpec((tm, width), lambda i: (i, 1)),
                  tab, tab],
        out_specs=[ospec, ospec],
        out_shape=[out, out],
        compiler_params=_cparams(("parallel",)),
        name="ret_prep",
    )(p, p, cos, sin)


def _ret_kernel(dec_ref, qf_ref, kf_ref, vf_ref, qb_ref, kb_ref, vb_ref, s0_ref,
                of_ref, ob_ref, sfin_ref, st_ref, dm_ref, rd_ref, *, hb):
    c = RET_CHUNK
    s = pl.program_id(2)

    @pl.when(s == 0)
    def _():
        st_ref[...] = s0_ref[...]
        row = lax.broadcasted_iota(jnp.int32, (c, c), 0).astype(F32)
        col = lax.broadcasted_iota(jnp.int32, (c, c), 1).astype(F32)
        for d in range(2):
            for hh in range(hb):
                ld = -jnp.exp(dec_ref[d, hh])
                if d == 0:
                    dist, keep = row - col, row >= col
                    q_pow, k_pow = row + 1.0, (c - 1.0) - row
                else:
                    dist, keep = col - row, col > row
                    q_pow, k_pow = c - row, row
                dm_ref[d, hh] = jnp.where(keep, jnp.exp(ld * jnp.maximum(dist, 0.0)), 0.0)
                rd_ref[d, hh, 0] = jnp.exp(ld * q_pow)
                rd_ref[d, hh, 1] = jnp.exp(ld * k_pow)

    streams = ((qf_ref, kf_ref, vf_ref, of_ref), (qb_ref, kb_ref, vb_ref, ob_ref))
    for d, (q_ref, k_ref, v_ref, o_ref) in enumerate(streams):
        for hh in range(hb):
            cols = slice(hh * RET_DIM, (hh + 1) * RET_DIM)
            q = q_ref[:, cols]
            k = k_ref[:, cols]
            v = v_ref[:, cols]
            state = st_ref[d, hh]
            att = _dot_nt(q, k) * dm_ref[d, hh]
            o = _dot(att.astype(BF16), v) + rd_ref[d, hh, 0] * _dot(q, state.astype(BF16))
            kt = (k.astype(F32) * rd_ref[d, hh, 1]).astype(BF16)
            chunk_decay = jnp.exp(-jnp.exp(dec_ref[d, hh]) * float(c))
            st_ref[d, hh] = state * chunk_decay + _dot_tn(kt, v)
            o_ref[:, cols] = o

    @pl.when(s == pl.num_programs(2) - 1)
    def _():
        sfin_ref[...] = st_ref[...]


def retention(q, k, p, dec, s0, hb=4):
    b, n, w = q.shape
    h = w // RET_DIM
    c = RET_CHUNK
    nc = n // c
    v0 = 2 * (h // hb)
    fw = pl.BlockSpec((None, c, hb * RET_DIM), lambda bb, g, s: (bb, s, g))
    bw = pl.BlockSpec((None, c, hb * RET_DIM), lambda bb, g, s: (bb, nc - 1 - s, g))
    vfw = pl.BlockSpec((None, c, hb * RET_DIM), lambda bb, g, s: (bb, s, v0 + g))
    vbw = pl.BlockSpec((None, c, hb * RET_DIM), lambda bb, g, s: (bb, nc - 1 - s, v0 + g))
    st = pl.BlockSpec((None, 2, hb, RET_DIM, RET_DIM), lambda bb, g, s: (bb, 0, g, 0, 0))
    return pl.pallas_call(
        functools.partial(_ret_kernel, hb=hb),
        grid=(b, h // hb, nc),
        in_specs=[pl.BlockSpec((2, hb, 1, RET_DIM), lambda bb, g, s: (0, g, 0, 0)),
                  fw, fw, vfw, bw, bw, vbw, st],
        out_specs=[fw, bw, st],
        out_shape=[jax.ShapeDtypeStruct((b, n, w), F32),
                   jax.ShapeDtypeStruct((b, n, w), F32),
                   jax.ShapeDtypeStruct(s0.shape, F32)],
        scratch_shapes=[pltpu.VMEM((2, hb, RET_DIM, RET_DIM), F32),
                        pltpu.VMEM((2, hb, c, c), F32),
                        pltpu.VMEM((2, hb, 2, c, RET_DIM), F32)],
        compiler_params=_cparams(("parallel", "parallel", "arbitrary")),
        name="retention",
    )(dec, q, k, p, q, k, p, s0)


def _ret_out_kernel(of_ref, ob_ref, g_ref, o_ref):
    for h in range(of_ref.shape[1] // RET_DIM):
        cols = slice(h * RET_DIM, (h + 1) * RET_DIM)
        o = of_ref[:, cols] + ob_ref[:, cols]
        ms = jnp.mean(o * o, axis=-1, keepdims=True)
        g = g_ref[:, cols].astype(F32)
        o_ref[:, cols] = (g * _sigmoid(g) * (o * lax.rsqrt(ms + EPS))).astype(o_ref.dtype)


def ret_out(o_f, o_b, p, tm=256):
    m, w = o_f.shape
    spec = pl.BlockSpec((tm, w), lambda i: (i, 0))
    return pl.pallas_call(
        _ret_out_kernel,
        grid=(m // tm,),
        in_specs=[spec, spec, pl.BlockSpec((tm, w), lambda i: (i, 3))],
        out_specs=spec,
        out_shape=jax.ShapeDtypeStruct((m, w), BF16),
        compiler_params=_cparams(("parallel",)),
        name="ret_out",
    )(o_f, o_b, p)


def _rope_tables(rows, head_dim):
    row = jnp.repeat(jnp.arange(rows), GRID_W).astype(F32)
    col = jnp.tile(jnp.arange(GRID_W), rows).astype(F32)
    n_freq = head_dim // 4
    inv_freq = ROPE_THETA ** (-jnp.arange(n_freq, dtype=F32) / n_freq)
    ang = jnp.concatenate([row[:, None] * inv_freq, col[:, None] * inv_freq], axis=-1)
    sign = jnp.tile(jnp.array([-1.0, 1.0], F32), head_dim // 2)
    return (jnp.repeat(jnp.cos(ang), 2, axis=-1),
            jnp.repeat(jnp.sin(ang), 2, axis=-1) * sign)


def _row_tile(m, pref):
    return pref if m % pref == 0 else m


def _ffn(h, mods, norm_gain, wg, wu, wd, gate_up_casts=(), down_casts=()):
    sh2, sc2, g2 = mods
    m = h.shape[0]
    hn = norm_mod(h, norm_gain, sc2, sh2)
    act, cast1 = matmul_swiglu(hn, wg, wu, 0, tm=_row_tile(m, 2048), tn=256,
                               casts=gate_up_casts)
    out, cast2 = matmul_resid([act], wd, 0, h, g2, tm=512, tn=512, casts=down_casts)
    return out, cast1, cast2


def kernel(x, c, ctx, c_ctx, norm_mix, norm_ffn, mod_down, mod_up, mod_bias,
           ffn_gate, ffn_up, ffn_down, ev_w_in, ev_w_out, sgu_norm, sgu_w, sgu_b,
           q_norm, k_norm, ret_w_in, ret_w_out, ret_decay_fwd, ret_decay_bwd):
    b, n, d = x.shape
    n_ctx = ctx.shape[1]
    depth = norm_mix.shape[0]
    a_width = sgu_w.shape[1] * SGU_GROUP
    n_q = (ev_w_out.shape[1] - a_width) // HEAD_DIM
    ret_w = ret_w_out.shape[1]
    ret_heads = ret_w // RET_DIM

    cv = jnp.concatenate([c, c_ctx[None], jnp.zeros((8 - b - 1, d), F32)], axis=0)
    mods = adaln_all(cv, mod_down, mod_up, mod_bias)

    cos_a, sin_a = _rope_tables(n // GRID_W, HEAD_DIM)
    cos_r, sin_r = _rope_tables(n // GRID_W, RET_DIM)

    def first(w):
        return w[:1].astype(BF16)

    mix_in, mix_out = first(ev_w_in), first(ev_w_out)
    wg = wu = wd = None

    h_lat = x.reshape(b * n, d)
    h_ctx = ctx.reshape(b * n_ctx, d)
    for i in range(depth):
        last = i == depth - 1
        j = i // 2
        lat_mod = [mods[i, :b, t * d:(t + 1) * d].reshape(b, 1, d) for t in range(N_MOD)]
        ctx_mod = [mods[i, b:b + 1, t * d:(t + 1) * d].reshape(1, 1, d) for t in range(N_MOD)]
        xn = norm_mod(h_lat, norm_mix[i], lat_mod[1], lat_mod[0])
        cn = norm_mod(h_ctx, norm_mix[i], ctx_mod[1], ctx_mod[0])
        if i % 2 == 0:
            in_casts = [(ffn_gate, 0), (ffn_up, 0)] if i == 0 else []
            out_casts = [(ffn_down, 0)] if i == 0 else []
            p_l, cast_in = matmul(xn, mix_in, 0, BF16, tm=1024, tn=512, casts=in_casts)
            p_c, _ = matmul(cn, mix_in, 0, BF16, tm=b * n_ctx, tn=512)
            ql, kl, vl = qkv_prep(p_l, q_norm[j], k_norm[j], cos_a, sin_a,
                                  2 * a_width, n_q, KV_HEADS, rope=True)
            qc, kc, vc = qkv_prep(p_c, q_norm[j], k_norm[j], cos_a, sin_a,
                                  2 * a_width, n_q, KV_HEADS, rope=False)
            kw = KV_HEADS * HEAD_DIM
            k_all = jnp.concatenate([kc.reshape(b, n_ctx, kw), kl.reshape(b, n, kw)], axis=1)
            v_all = jnp.concatenate([vc.reshape(b, n_ctx, kw), vl.reshape(b, n, kw)], axis=1)
            tk = next(t for t in (384, 512, 256) if (n_ctx + n) % t == 0)
            att_l = flash_attention(ql.reshape(b, n, -1), k_all, v_all, tq=256, tk=tk)
            sgu_l = sgu(p_l, sgu_norm[j], sgu_w[j], sgu_b[j])
            h_lat, cast_out = matmul_resid([sgu_l, att_l.reshape(b * n, -1)], mix_out, 0, h_lat,
                                           lat_mod[2], tm=1024, tn=512, casts=out_casts)
            if i == 0:
                (wg, wu), (wd,) = cast_in, cast_out
            if not last:
                att_c = flash_attention(qc.reshape(b, n_ctx, -1), kc.reshape(b, n_ctx, kw),
                                        vc.reshape(b, n_ctx, kw), tq=n_ctx, tk=n_ctx)
                sgu_c = sgu(p_c, sgu_norm[j], sgu_w[j], sgu_b[j])
                h_ctx, _ = matmul_resid([sgu_c, att_c.reshape(b * n_ctx, -1)], mix_out, 0,
                                        h_ctx, ctx_mod[2], tm=b * n_ctx, tn=512)
        else:
            p_l, _ = matmul(xn, mix_in, 0, BF16, tm=1024, tn=512)
            p_c, _ = matmul(cn, mix_in, 0, BF16, tm=b * n_ctx, tn=512)
            ql, kl = ret_prep(p_l, cos_r, sin_r, ret_w, rope=True)
            qc, kc = ret_prep(p_c, cos_r, sin_r, ret_w, rope=False)
            dec = jnp.stack([ret_decay_fwd[j], ret_decay_bwd[j]]).astype(F32)
            dec = jnp.broadcast_to(dec[:, :, None, None], (2, ret_heads, 1, RET_DIM))
            s0 = jnp.zeros((b, 2, ret_heads, RET_DIM, RET_DIM), F32)
            shp_c, shp_l = (b, n_ctx, ret_w), (b, n, ret_w)
            ocf, ocb, s_ctx = retention(qc.reshape(shp_c), kc.reshape(shp_c),
                                        p_c.reshape(b, n_ctx, -1), dec, s0)
            olf, olb, _ = retention(ql.reshape(shp_l), kl.reshape(shp_l),
                                    p_l.reshape(b, n, -1), dec, s_ctx)
            y_l = ret_out(olf.reshape(b * n, ret_w), olb.reshape(b * n, ret_w), p_l)
            h_lat, _ = matmul_resid([y_l], mix_out, 0, h_lat, lat_mod[2], tm=1024, tn=512)
            if not last:
                y_c = ret_out(ocf.reshape(b * n_ctx, ret_w), ocb.reshape(b * n_ctx, ret_w), p_c)
                h_ctx, _ = matmul_resid([y_c], mix_out, 0, h_ctx, ctx_mod[2],
                                        tm=b * n_ctx, tn=512)

        if last:
            h_lat, _, _ = _ffn(h_lat, lat_mod[3:], norm_ffn[i], wg, wu, wd)
        else:
            nj = (i + 1) // 2
            nxt_in, nxt_out = (ev_w_in, ev_w_out) if (i + 1) % 2 == 0 else (ret_w_in, ret_w_out)
            h_lat, cast1, cast2 = _ffn(
                h_lat, lat_mod[3:], norm_ffn[i], wg, wu, wd,
                gate_up_casts=[(ffn_down, i + 1), (ffn_gate, i + 1), (nxt_in, nj)],
                down_casts=[(ffn_up, i + 1), (nxt_out, nj)])
            h_ctx, _, _ = _ffn(h_ctx, ctx_mod[3:], norm_ffn[i], wg, wu, wd)
            (wd, wg, mix_in), (wu, mix_out) = cast1, cast2
    return h_lat.reshape(b, n, d)
```

```python
import functools

import jax
import jax.numpy as jnp
from jax import lax
from jax.experimental import pallas as pl
from jax.experimental.pallas import tpu as pltpu

F32 = jnp.float32
BF16 = jnp.bfloat16

EPS = 1e-6
GRID_W = 64
N_MOD = 6
ROPE_THETA = 10000.0
SGU_CHUNK = 128
SGU_GROUP = 128
HEAD_DIM = 128
KV_HEADS = 4
RET_DIM = 256
RET_CHUNK = 256
LANES = 128
BF16_SUBLANES = 16
LOG2E = 1.4426950408889634
FLASH_UNROLL = 10
FLASH_ROWS = 64
VMEM_LIMIT = 56 * 1024 * 1024


def _cparams(sem):
    return pltpu.CompilerParams(dimension_semantics=sem, vmem_limit_bytes=VMEM_LIMIT)


def _sigmoid(x):
    return 1.0 / (1.0 + jnp.exp(-x))


def _gelu(x):
    return 0.5 * x * (1.0 + jnp.tanh(0.7978845608028654 * (x + 0.044715 * (x * x * x))))


def _dot(a, b):
    return jnp.dot(a, b, preferred_element_type=F32)


def _dot_nt(a, b):
    return lax.dot_general(a, b, (((1,), (1,)), ((), ())), preferred_element_type=F32)


def _dot_tn(a, b):
    return lax.dot_general(a, b, (((0,), (0,)), ((), ())), preferred_element_type=F32)


def _adaln_kernel(cv_ref, down_ref, up_ref, bias_ref, o_ref, hd_ref):
    @pl.when(pl.program_id(1) == 0)
    def _():
        cv = cv_ref[...]
        hd_ref[...] = _dot(cv * _sigmoid(cv), down_ref[...])

    o_ref[...] = _dot(hd_ref[...], up_ref[...]) + bias_ref[...]


def adaln_all(cv, down, up, bias, tn=2048):
    depth, d, rank = down.shape
    nout = up.shape[-1]
    rows = cv.shape[0]
    return pl.pallas_call(
        _adaln_kernel,
        grid=(depth, nout // tn),
        in_specs=[
            pl.BlockSpec((rows, d), lambda l, j: (0, 0)),
            pl.BlockSpec((None, d, rank), lambda l, j: (l, 0, 0)),
            pl.BlockSpec((None, rank, tn), lambda l, j: (l, 0, j)),
            pl.BlockSpec((None, 1, tn), lambda l, j: (l, 0, j)),
        ],
        out_specs=pl.BlockSpec((None, rows, tn), lambda l, j: (l, 0, j)),
        out_shape=jax.ShapeDtypeStruct((depth, rows, nout), F32),
        scratch_shapes=[pltpu.VMEM((rows, rank), F32)],
        compiler_params=_cparams(("arbitrary", "arbitrary")),
        name="adaln",
    )(cv, down, up, bias.reshape(depth, 1, nout))


def _norm_mod_kernel(x_ref, gain_ref, scale_ref, shift_ref, o_ref):
    x = x_ref[...]
    ms = jnp.mean(x * x, axis=-1, keepdims=True)
    xn = x * lax.rsqrt(ms + EPS) * gain_ref[...]
    o_ref[...] = (xn * (1.0 + scale_ref[...]) + shift_ref[...]).astype(o_ref.dtype)


def norm_mod(x, gain, scale, shift, tm=512):
    m, d = x.shape
    nb = scale.shape[0]
    assert (m // tm) % nb == 0
    tpb = (m // tm) // nb
    vec = pl.BlockSpec((None, 1, d), lambda i: (i // tpb, 0, 0))
    return pl.pallas_call(
        _norm_mod_kernel,
        grid=(m // tm,),
        in_specs=[pl.BlockSpec((tm, d), lambda i: (i, 0)),
                  pl.BlockSpec((1, d), lambda i: (0, 0)), vec, vec],
        out_specs=pl.BlockSpec((tm, d), lambda i: (i, 0)),
        out_shape=jax.ShapeDtypeStruct((m, d), BF16),
        compiler_params=_cparams(("parallel",)),
        name="norm_mod",
    )(x, gain.reshape(1, d), scale, shift)


def _mm_store_kernel(a_ref, w_ref, *refs, n_c):
    cast_src, o_ref, cast_dst = refs[:n_c], refs[n_c], refs[n_c + 1:]
    o_ref[...] = _dot(a_ref[...], w_ref[...]).astype(o_ref.dtype)
    _run_casts(cast_src, cast_dst)


def matmul(a, w, layer, out_dtype, tm, tn, casts=()):
    m, k = a.shape
    n = w.shape[2]
    c_in, c_out, c_shapes, c_args = _cast_jobs(casts, m // tm, n // tn)
    outs = pl.pallas_call(
        functools.partial(_mm_store_kernel, n_c=len(casts)),
        grid=(m // tm, n // tn),
        in_specs=[pl.BlockSpec((tm, k), lambda i, j: (i, 0)),
                  pl.BlockSpec((None, k, tn), lambda i, j: (layer, 0, j))] + c_in,
        out_specs=[pl.BlockSpec((tm, tn), lambda i, j: (i, j))] + c_out,
        out_shape=[jax.ShapeDtypeStruct((m, n), out_dtype)] + c_shapes,
        compiler_params=_cparams(("arbitrary", "arbitrary")),
        name="matmul",
    )(a, w, *c_args)
    return outs[0], outs[1:]


def _cast_jobs(casts, steps_i, steps_j):
    in_specs, out_specs, out_shapes, args = [], [], [], []
    for w, layer in casts:
        _, k, c = w.shape
        rows = next(r for r in range(BF16_SUBLANES, k + 1, BF16_SUBLANES)
                    if k % r == 0 and k // r <= steps_i * steps_j)
        last = k // rows - 1
        in_specs.append(pl.BlockSpec(
            (None, rows, c),
            lambda i, j, l=layer, e=last: (l, jnp.minimum(i * steps_j + j, e), 0)))
        out_specs.append(pl.BlockSpec(
            (None, rows, c), lambda i, j, e=last: (0, jnp.minimum(i * steps_j + j, e), 0)))
        out_shapes.append(jax.ShapeDtypeStruct((1, k, c), BF16))
        args.append(w)
    return in_specs, out_specs, out_shapes, args


def _run_casts(src_refs, dst_refs):
    for src, dst in zip(src_refs, dst_refs):
        dst[...] = src[...].astype(dst.dtype)


def _mm_resid_kernel(*refs, n_a, n_c):
    a_refs, w_refs = refs[:n_a], refs[n_a:2 * n_a]
    resid_ref, gate_ref = refs[2 * n_a:2 * n_a + 2]
    cast_src = refs[2 * n_a + 2:2 * n_a + 2 + n_c]
    o_ref = refs[2 * n_a + 2 + n_c]
    cast_dst = refs[2 * n_a + 3 + n_c:]
    acc = _dot(a_refs[0][...], w_refs[0][...])
    for a_ref, w_ref in zip(a_refs[1:], w_refs[1:]):
        acc = acc + _dot(a_ref[...], w_ref[...])
    o_ref[...] = resid_ref[...] + gate_ref[...] * acc
    _run_casts(cast_src, cast_dst)


def matmul_resid(a_list, w, layer, resid, gate, tm, tn, casts=()):
    m, n = resid.shape
    nb = gate.shape[0]
    assert (m // tm) % nb == 0
    tpb = (m // tm) // nb
    n_a = len(a_list)
    c_in, c_out, c_shapes, c_args = _cast_jobs(casts, m // tm, n // tn)
    in_specs, w_args, off = [], [], 0
    for a in a_list:
        in_specs.append(pl.BlockSpec((tm, a.shape[1]), lambda i, j: (i, 0)))
    for a in a_list:
        ka = a.shape[1]
        assert off % ka == 0
        in_specs.append(pl.BlockSpec((None, ka, tn), lambda i, j, r=off // ka: (layer, r, j)))
        w_args.append(w)
        off += ka
    assert off == w.shape[1]
    in_specs += [pl.BlockSpec((tm, tn), lambda i, j: (i, j)),
                 pl.BlockSpec((None, 1, tn), lambda i, j: (i // tpb, 0, j))]
    outs = pl.pallas_call(
        functools.partial(_mm_resid_kernel, n_a=n_a, n_c=len(casts)),
        grid=(m // tm, n // tn),
        in_specs=in_specs + c_in,
        out_specs=[pl.BlockSpec((tm, tn), lambda i, j: (i, j))] + c_out,
        out_shape=[jax.ShapeDtypeStruct((m, n), F32)] + c_shapes,
        compiler_params=_cparams(("arbitrary", "arbitrary")),
        name="matmul_resid",
    )(*a_list, *w_args, resid, gate, *c_args)
    return outs[0], outs[1:]


def _mm_swiglu_kernel(a_ref, wg_ref, wu_ref, *refs, n_c):
    cast_src, o_ref, cast_dst = refs[:n_c], refs[n_c], refs[n_c + 1:]
    a = a_ref[...]
    g = _dot(a, wg_ref[...])
    u = _dot(a, wu_ref[...])
    o_ref[...] = (g * _sigmoid(g) * u).astype(o_ref.dtype)
    _run_casts(cast_src, cast_dst)


def matmul_swiglu(a, wg, wu, layer, tm, tn, casts=()):
    m, k = a.shape
    n = wg.shape[2]
    wspec = pl.BlockSpec((None, k, tn), lambda i, j: (layer, 0, j))
    c_in, c_out, c_shapes, c_args = _cast_jobs(casts, m // tm, n // tn)
    outs = pl.pallas_call(
        functools.partial(_mm_swiglu_kernel, n_c=len(casts)),
        grid=(m // tm, n // tn),
        in_specs=[pl.BlockSpec((tm, k), lambda i, j: (i, 0)), wspec, wspec] + c_in,
        out_specs=[pl.BlockSpec((tm, tn), lambda i, j: (i, j))] + c_out,
        out_shape=[jax.ShapeDtypeStruct((m, n), BF16)] + c_shapes,
        compiler_params=_cparams(("arbitrary", "arbitrary")),
        name="matmul_swiglu",
    )(a, wg, wu, *c_args)
    return outs[0], outs[1:]


def _rope_slab(x, cos, sin_signed, even):
    swapped = jnp.where(even, pltpu.roll(x, LANES - 1, 1), pltpu.roll(x, 1, 1))
    return x * cos + swapped * sin_signed


def _even_lanes(rows):
    return (lax.broadcasted_iota(jnp.int32, (rows, LANES), 1) % 2) == 0


def _sgu_kernel(u_ref, v_ref, gain_ref, w_ref, b_ref, o_ref):
    tm = u_ref.shape[0]
    groups = w_ref.shape[0]
    v = _gelu(v_ref[...].astype(F32))
    ms = jnp.mean(v * v, axis=-1, keepdims=True)
    vn = (v * lax.rsqrt(ms + EPS) * gain_ref[...]).astype(BF16)
    bias = b_ref[...]
    for c in range(tm // SGU_CHUNK):
        rows = slice(c * SGU_CHUNK, (c + 1) * SGU_CHUNK)
        for g in range(groups):
            cols = slice(g * SGU_GROUP, (g + 1) * SGU_GROUP)
            mixed = _dot(w_ref[g], vn[rows, cols]) + bias[:, g:g + 1]
            u = _gelu(u_ref[rows, cols].astype(F32))
            o_ref[rows, cols] = (u * mixed).astype(o_ref.dtype)


def sgu(p, gain, w_s, b_s, tm=256):
    m = p.shape[0]
    groups = w_s.shape[0]
    width = groups * SGU_GROUP
    return pl.pallas_call(
        _sgu_kernel,
        grid=(m // tm,),
        in_specs=[pl.BlockSpec((tm, width), lambda i: (i, 0)),
                  pl.BlockSpec((tm, width), lambda i: (i, 1)),
                  pl.BlockSpec((1, width), lambda i: (0, 0)),
                  pl.BlockSpec((groups, SGU_CHUNK, SGU_CHUNK), lambda i: (0, 0, 0)),
                  pl.BlockSpec((SGU_CHUNK, groups), lambda i: (0, 0))],
        out_specs=pl.BlockSpec((tm, width), lambda i: (i, 0)),
        out_shape=jax.ShapeDtypeStruct((m, width), BF16),
        compiler_params=_cparams(("parallel",)),
        name="sgu",
    )(p, p, gain.reshape(1, width), w_s.astype(BF16), b_s.T)


def _qkv_prep_kernel(q_ref, k_ref, v_ref, qg_ref, kg_ref, cos_ref, sin_ref,
                     qo_ref, ko_ref, vo_ref, *, rope, q_scale):
    tm = q_ref.shape[0]
    even = _even_lanes(tm)
    cos = cos_ref[...]
    sin = sin_ref[...]

    def prep(x, gain):
        ms = jnp.mean(x * x, axis=-1, keepdims=True)
        xn = x * lax.rsqrt(ms + EPS) * gain
        if rope:
            xn = _rope_slab(xn, cos, sin, even)
        return xn

    qg = qg_ref[...]
    kg = kg_ref[...]
    for h in range(q_ref.shape[1] // HEAD_DIM):
        cols = slice(h * HEAD_DIM, (h + 1) * HEAD_DIM)
        qo_ref[:, cols] = (prep(q_ref[:, cols].astype(F32), qg) * q_scale).astype(qo_ref.dtype)
    for h in range(k_ref.shape[1] // HEAD_DIM):
        cols = slice(h * HEAD_DIM, (h + 1) * HEAD_DIM)
        ko_ref[:, cols] = prep(k_ref[:, cols].astype(F32), kg).astype(ko_ref.dtype)
    vo_ref[...] = v_ref[...].astype(vo_ref.dtype)


def qkv_prep(p, q_gain, k_gain, cos, sin, col0, n_q, n_kv, rope, tm=256):
    m = p.shape[0]
    qw, kw = n_q * HEAD_DIM, n_kv * HEAD_DIM
    assert col0 % qw == 0 and (col0 + qw) % kw == 0
    pos_tiles = cos.shape[0] // tm
    tab = pl.BlockSpec((tm, HEAD_DIM), lambda i: (i % pos_tiles, 0))
    kern = functools.partial(_qkv_prep_kernel, rope=rope, q_scale=HEAD_DIM ** -0.5 * LOG2E)
    return pl.pallas_call(
        kern,
        grid=(m // tm,),
        in_specs=[pl.BlockSpec((tm, qw), lambda i: (i, col0 // qw)),
                  pl.BlockSpec((tm, kw), lambda i: (i, (col0 + qw) // kw)),
                  pl.BlockSpec((tm, kw), lambda i: (i, (col0 + qw) // kw + 1)),
                  pl.BlockSpec((1, HEAD_DIM), lambda i: (0, 0)),
                  pl.BlockSpec((1, HEAD_DIM), lambda i: (0, 0)),
                  tab, tab],
        out_specs=[pl.BlockSpec((tm, qw), lambda i: (i, 0)),
                   pl.BlockSpec((tm, kw), lambda i: (i, 0)),
                   pl.BlockSpec((tm, kw), lambda i: (i, 0))],
        out_shape=[jax.ShapeDtypeStruct((m, qw), BF16),
                   jax.ShapeDtypeStruct((m, kw), BF16),
                   jax.ShapeDtypeStruct((m, kw), BF16)],
        compiler_params=_cparams(("parallel",)),
        name="qkv_prep",
    )(p, p, p, q_gain.reshape(1, HEAD_DIM), k_gain.reshape(1, HEAD_DIM), cos, sin)


def _flash_kernel(q_ref, k_ref, v_ref, o_ref, sa_ref, sb_ref, pa_ref, pb_ref,
                  m_ref, l_ref, acc_ref, alpha_ref, *, tk, group):
    tq = q_ref.shape[0]
    nk = k_ref.shape[0] // tk
    ncb = tk // LANES
    q = jnp.concatenate(
        [q_ref[:, g * HEAD_DIM:(g + 1) * HEAD_DIM] for g in range(group)], axis=0)
    m_ref[...] = jnp.full(m_ref.shape, -jnp.inf, F32)
    l_ref[...] = jnp.zeros(l_ref.shape, F32)
    acc_ref[...] = jnp.zeros(acc_ref.shape, F32)
    s_bufs = (sa_ref, sb_ref)
    p_bufs = (pa_ref, pb_ref)

    def scores(j, s_ref):
        off = pl.multiple_of(j * tk, tk)
        s_ref[...] = _dot_nt(q, k_ref[pl.ds(off, tk), :])

    def pv(j, p_ref):
        off = pl.multiple_of(j * tk, tk)
        acc_ref[...] += _dot(p_ref[...], v_ref[pl.ds(off, tk), :])

    def softmax(s_ref, p_ref):
        for r0 in range(0, group * tq, FLASH_ROWS):
            rows = slice(r0, r0 + FLASH_ROWS)
            blocks = [s_ref[rows, cb * LANES:(cb + 1) * LANES] for cb in range(ncb)]
            mx = blocks[0]
            for blk in blocks[1:]:
                mx = jnp.maximum(mx, blk)
            m_prev = m_ref[rows, :]
            m_new = jnp.maximum(
                m_prev, jnp.broadcast_to(jnp.max(mx, axis=-1, keepdims=True), mx.shape))
            alpha = jnp.exp2(m_prev - m_new)
            psum = None
            for cb, blk in enumerate(blocks):
                pb = jnp.exp2(blk - m_new)
                psum = pb if psum is None else psum + pb
                p_ref[rows, cb * LANES:(cb + 1) * LANES] = pb.astype(BF16)
            l_ref[rows, :] = alpha * l_ref[rows, :] + psum
            alpha_ref[rows, :] = alpha
            m_ref[rows, :] = m_new

    def stage(j, parity, has_next):
        pv(j - 1, p_bufs[1 - parity])
        if has_next:
            scores(j + 1, s_bufs[1 - parity])
        softmax(s_bufs[parity], p_bufs[parity])
        acc_ref[...] = acc_ref[...] * alpha_ref[...]

    scores(0, sa_ref)
    if nk > 1:
        scores(1, sb_ref)
    softmax(sa_ref, pa_ref)

    def body(jj, carry):
        j = FLASH_UNROLL * jj + 1
        for t in range(FLASH_UNROLL):
            stage(j + t, (1 + t) % 2, True)
        return carry

    trips = max(nk - 2, 0) // FLASH_UNROLL
    lax.fori_loop(0, trips, body, 0)
    for j in range(FLASH_UNROLL * trips + 1, nk):
        stage(j, j % 2, j + 1 < nk)
    pv(nk - 1, p_bufs[(nk - 1) % 2])

    out = acc_ref[...] / jnp.sum(l_ref[...], axis=-1, keepdims=True)
    for g in range(group):
        o_ref[:, g * HEAD_DIM:(g + 1) * HEAD_DIM] = out[g * tq:(g + 1) * tq].astype(o_ref.dtype)


def flash_attention(q, k, v, tq, tk):
    b, n, qw = q.shape
    s = k.shape[1]
    kvh = k.shape[2] // HEAD_DIM
    group = qw // HEAD_DIM // kvh
    gw = group * HEAD_DIM
    assert s % tk == 0 and n % tq == 0
    return pl.pallas_call(
        functools.partial(_flash_kernel, tk=tk, group=group),
        grid=(b, kvh, n // tq),
        in_specs=[pl.BlockSpec((None, tq, gw), lambda bb, h, i: (bb, i, h)),
                  pl.BlockSpec((None, s, HEAD_DIM), lambda bb, h, i: (bb, 0, h)),
                  pl.BlockSpec((None, s, HEAD_DIM), lambda bb, h, i: (bb, 0, h))],
        out_specs=pl.BlockSpec((None, tq, gw), lambda bb, h, i: (bb, i, h)),
        out_shape=jax.ShapeDtypeStruct((b, n, qw), BF16),
        scratch_shapes=[pltpu.VMEM((group * tq, tk), F32),
                        pltpu.VMEM((group * tq, tk), F32),
                        pltpu.VMEM((group * tq, tk), BF16),
                        pltpu.VMEM((group * tq, tk), BF16),
                        pltpu.VMEM((group * tq, LANES), F32),
                        pltpu.VMEM((group * tq, LANES), F32),
                        pltpu.VMEM((group * tq, HEAD_DIM), F32),
                        pltpu.VMEM((group * tq, HEAD_DIM), F32)],
        compiler_params=_cparams(("parallel", "parallel", "arbitrary")),
        name="flash_attention",
    )(q, k, v)


def _ret_prep_kernel(q_ref, k_ref, cos_ref, sin_ref, qo_ref, ko_ref, *, rope, k_scale):
    tm = q_ref.shape[0]
    even = _even_lanes(tm)
    slabs_per_head = RET_DIM // LANES
    for sl in range(q_ref.shape[1] // LANES):
        cols = slice(sl * LANES, (sl + 1) * LANES)
        tcols = slice((sl % slabs_per_head) * LANES, (sl % slabs_per_head + 1) * LANES)
        q = q_ref[:, cols].astype(F32)
        k = k_ref[:, cols].astype(F32) * k_scale
        if rope:
            cos = cos_ref[:, tcols]
            sin = sin_ref[:, tcols]
            q = _rope_slab(q, cos, sin, even)
            k = _rope_slab(k, cos, sin, even)
        qo_ref[:, cols] = q.astype(qo_ref.dtype)
        ko_ref[:, cols] = k.astype(ko_ref.dtype)


def ret_prep(p, cos, sin, width, rope, tm=256):
    m = p.shape[0]
    pos_tiles = cos.shape[0] // tm
    tab = pl.BlockSpec((tm, RET_DIM), lambda i: (i % pos_tiles, 0))
    out = jax.ShapeDtypeStruct((m, width), BF16)
    ospec = pl.BlockSpec((tm, width), lambda i: (i, 0))
    return pl.pallas_call(
        functools.partial(_ret_prep_kernel, rope=rope, k_scale=RET_DIM ** -0.5),
        grid=(m // tm,),
        in_specs=[pl.BlockSpec((tm, width), lambda i: (i, 0)),
                  pl.BlockSpec((tm, width), lambda i: (i, 1)),
                  tab, tab],
        out_specs=[ospec, ospec],
        out_shape=[out, out],
        compiler_params=_cparams(("parallel",)),
        name="ret_prep",
    )(p, p, cos, sin)


def _ret_kernel(dec_ref, qf_ref, kf_ref, vf_ref, qb_ref, kb_ref, vb_ref, s0_ref,
                of_ref, ob_ref, sfin_ref, st_ref, dm_ref, rd_ref, *, hb):
    c = RET_CHUNK
    s = pl.program_id(2)

    @pl.when(s == 0)
    def _():
        st_ref[...] = s0_ref[...]
        row = lax.broadcasted_iota(jnp.int32, (c, c), 0).astype(F32)
        col = lax.broadcasted_iota(jnp.int32, (c, c), 1).astype(F32)
        for d in range(2):
            for hh in range(hb):
                ld = -jnp.exp(dec_ref[d, hh])
                if d == 0:
                    dist, keep = row - col, row >= col
                    q_pow, k_pow = row + 1.0, (c - 1.0) - row
                else:
                    dist, keep = col - row, col > row
                    q_pow, k_pow = c - row, row
                dm_ref[d, hh] = jnp.where(keep, jnp.exp(ld * jnp.maximum(dist, 0.0)), 0.0)
                rd_ref[d, hh, 0] = jnp.exp(ld * q_pow)
                rd_ref[d, hh, 1] = jnp.exp(ld * k_pow)

    streams = ((qf_ref, kf_ref, vf_ref, of_ref), (qb_ref, kb_ref, vb_ref, ob_ref))
    for d, (q_ref, k_ref, v_ref, o_ref) in enumerate(streams):
        for hh in range(hb):
            cols = slice(hh * RET_DIM, (hh + 1) * RET_DIM)
            q = q_ref[:, cols]
            k = k_ref[:, cols]
            v = v_ref[:, cols]
            state = st_ref[d, hh]
            att = _dot_nt(q, k) * dm_ref[d, hh]
            o = _dot(att.astype(BF16), v) + rd_ref[d, hh, 0] * _dot(q, state.astype(BF16))
            kt = (k.astype(F32) * rd_ref[d, hh, 1]).astype(BF16)
            chunk_decay = jnp.exp(-jnp.exp(dec_ref[d, hh]) * float(c))
            st_ref[d, hh] = state * chunk_decay + _dot_tn(kt, v)
            o_ref[:, cols] = o

    @pl.when(s == pl.num_programs(2) - 1)
    def _():
        sfin_ref[...] = st_ref[...]


def retention(q, k, p, dec, s0, hb=4):
    b, n, w = q.shape
    h = w // RET_DIM
    c = RET_CHUNK
    nc = n // c
    v0 = 2 * (h // hb)
    fw = pl.BlockSpec((None, c, hb * RET_DIM), lambda bb, g, s: (bb, s, g))
    bw = pl.BlockSpec((None, c, hb * RET_DIM), lambda bb, g, s: (bb, nc - 1 - s, g))
    vfw = pl.BlockSpec((None, c, hb * RET_DIM), lambda bb, g, s: (bb, s, v0 + g))
    vbw = pl.BlockSpec((None, c, hb * RET_DIM), lambda bb, g, s: (bb, nc - 1 - s, v0 + g))
    st = pl.BlockSpec((None, 2, hb, RET_DIM, RET_DIM), lambda bb, g, s: (bb, 0, g, 0, 0))
    return pl.pallas_call(
        functools.partial(_ret_kernel, hb=hb),
        grid=(b, h // hb, nc),
        in_specs=[pl.BlockSpec((2, hb, 1, RET_DIM), lambda bb, g, s: (0, g, 0, 0)),
                  fw, fw, vfw, bw, bw, vbw, st],
        out_specs=[fw, bw, st],
        out_shape=[jax.ShapeDtypeStruct((b, n, w), F32),
                   jax.ShapeDtypeStruct((b, n, w), F32),
                   jax.ShapeDtypeStruct(s0.shape, F32)],
        scratch_shapes=[pltpu.VMEM((2, hb, RET_DIM, RET_DIM), F32),
                        pltpu.VMEM((2, hb, c, c), F32),
                        pltpu.VMEM((2, hb, 2, c, RET_DIM), F32)],
        compiler_params=_cparams(("parallel", "parallel", "arbitrary")),
        name="retention",
    )(dec, q, k, p, q, k, p, s0)


def _ret_out_kernel(of_ref, ob_ref, g_ref, o_ref):
    for h in range(of_ref.shape[1] // RET_DIM):
        cols = slice(h * RET_DIM, (h + 1) * RET_DIM)
        o = of_ref[:, cols] + ob_ref[:, cols]
        ms = jnp.mean(o * o, axis=-1, keepdims=True)
        g = g_ref[:, cols].astype(F32)
        o_ref[:, cols] = (g * _sigmoid(g) * (o * lax.rsqrt(ms + EPS))).astype(o_ref.dtype)


def ret_out(o_f, o_b, p, tm=256):
    m, w = o_f.shape
    spec = pl.BlockSpec((tm, w), lambda i: (i, 0))
    return pl.pallas_call(
        _ret_out_kernel,
        grid=(m // tm,),
        in_specs=[spec, spec, pl.BlockSpec((tm, w), lambda i: (i, 3))],
        out_specs=spec,
        out_shape=jax.ShapeDtypeStruct((m, w), BF16),
        compiler_params=_cparams(("parallel",)),
        name="ret_out",
    )(o_f, o_b, p)


def _rope_tables(rows, head_dim):
    row = jnp.repeat(jnp.arange(rows), GRID_W).astype(F32)
    col = jnp.tile(jnp.arange(GRID_W), rows).astype(F32)
    n_freq = head_dim // 4
    inv_freq = ROPE_THETA ** (-jnp.arange(n_freq, dtype=F32) / n_freq)
    ang = jnp.concatenate([row[:, None] * inv_freq, col[:, None] * inv_freq], axis=-1)
    sign = jnp.tile(jnp.array([-1.0, 1.0], F32), head_dim // 2)
    return (jnp.repeat(jnp.cos(ang), 2, axis=-1),
            jnp.repeat(jnp.sin(ang), 2, axis=-1) * sign)


def _row_tile(m, pref):
    return pref if m % pref == 0 else m


def _ffn(h, mods, norm_gain, wg, wu, wd, gate_up_casts=(), down_casts=()):
    sh2, sc2, g2 = mods
    m = h.shape[0]
    hn = norm_mod(h, norm_gain, sc2, sh2)
    act, cast1 = matmul_swiglu(hn, wg, wu, 0, tm=_row_tile(m, 2048), tn=256,
                               casts=gate_up_casts)
    out, cast2 = matmul_resid([act], wd, 0, h, g2, tm=512, tn=512, casts=down_casts)
    return out, cast1, cast2


def kernel(x, c, ctx, c_ctx, norm_mix, norm_ffn, mod_down, mod_up, mod_bias,
           ffn_gate, ffn_up, ffn_down, ev_w_in, ev_w_out, sgu_norm, sgu_w, sgu_b,
           q_norm, k_norm, ret_w_in, ret_w_out, ret_decay_fwd, ret_decay_bwd):
    b, n, d = x.shape
    n_ctx = ctx.shape[1]
    depth = norm_mix.shape[0]
    a_width = sgu_w.shape[1] * SGU_GROUP
    n_q = (ev_w_out.shape[1] - a_width) // HEAD_DIM
    ret_w = ret_w_out.shape[1]
    ret_heads = ret_w // RET_DIM

    cv = jnp.concatenate([c, c_ctx[None], jnp.zeros((8 - b - 1, d), F32)], axis=0)
    mods = adaln_all(cv, mod_down, mod_up, mod_bias)

    cos_a, sin_a = _rope_tables(n // GRID_W, HEAD_DIM)
    cos_r, sin_r = _rope_tables(n // GRID_W, RET_DIM)

    def first(w):
        return w[:1].astype(BF16)

    mix_in, mix_out = first(ev_w_in), first(ev_w_out)
    wg = wu = wd = None

    h_lat = x.reshape(b * n, d)
    h_ctx = ctx.reshape(b * n_ctx, d)
    for i in range(depth):
        last = i == depth - 1
        j = i // 2
        lat_mod = [mods[i, :b, t * d:(t + 1) * d].reshape(b, 1, d) for t in range(N_MOD)]
        ctx_mod = [mods[i, b:b + 1, t * d:(t + 1) * d].reshape(1, 1, d) for t in range(N_MOD)]
        xn = norm_mod(h_lat, norm_mix[i], lat_mod[1], lat_mod[0])
        cn = norm_mod(h_ctx, norm_mix[i], ctx_mod[1], ctx_mod[0])
        if i % 2 == 0:
            in_casts = [(ffn_gate, 0), (ffn_up, 0)] if i == 0 else []
            out_casts = [(ffn_down, 0)] if i == 0 else []
            p_l, cast_in = matmul(xn, mix_in, 0, BF16, tm=1024, tn=512, casts=in_casts)
            p_c, _ = matmul(cn, mix_in, 0, BF16, tm=b * n_ctx, tn=512)
            ql, kl, vl = qkv_prep(p_l, q_norm[j], k_norm[j], cos_a, sin_a,
                                  2 * a_width, n_q, KV_HEADS, rope=True)
            qc, kc, vc = qkv_prep(p_c, q_norm[j], k_norm[j], cos_a, sin_a,
                                  2 * a_width, n_q, KV_HEADS, rope=False)
            kw = KV_HEADS * HEAD_DIM
            k_all = jnp.concatenate([kc.reshape(b, n_ctx, kw), kl.reshape(b, n, kw)], axis=1)
            v_all = jnp.concatenate([vc.reshape(b, n_ctx, kw), vl.reshape(b, n, kw)], axis=1)
            tk = next(t for t in (384, 512, 256) if (n_ctx + n) % t == 0)
            att_l = flash_attention(ql.reshape(b, n, -1), k_all, v_all, tq=256, tk=tk)
            sgu_l = sgu(p_l, sgu_norm[j], sgu_w[j], sgu_b[j])
            h_lat, cast_out = matmul_resid([sgu_l, att_l.reshape(b * n, -1)], mix_out, 0, h_lat,
                                           lat_mod[2], tm=1024, tn=512, casts=out_casts)
            if i == 0:
                (wg, wu), (wd,) = cast_in, cast_out
            if not last:
                att_c = flash_attention(qc.reshape(b, n_ctx, -1), kc.reshape(b, n_ctx, kw),
                                        vc.reshape(b, n_ctx, kw), tq=n_ctx, tk=n_ctx)
                sgu_c = sgu(p_c, sgu_norm[j], sgu_w[j], sgu_b[j])
                h_ctx, _ = matmul_resid([sgu_c, att_c.reshape(b * n_ctx, -1)], mix_out, 0,
                                        h_ctx, ctx_mod[2], tm=b * n_ctx, tn=512)
        else:
            p_l, _ = matmul(xn, mix_in, 0, BF16, tm=1024, tn=1024)
            p_c, _ = matmul(cn, mix_in, 0, BF16, tm=b * n_ctx, tn=1024)
            ql, kl = ret_prep(p_l, cos_r, sin_r, ret_w, rope=True)
            qc, kc = ret_prep(p_c, cos_r, sin_r, ret_w, rope=False)
            dec = jnp.stack([ret_decay_fwd[j], ret_decay_bwd[j]]).astype(F32)
            dec = jnp.broadcast_to(dec[:, :, None, None], (2, ret_heads, 1, RET_DIM))
            s0 = jnp.zeros((b, 2, ret_heads, RET_DIM, RET_DIM), F32)
            shp_c, shp_l = (b, n_ctx, ret_w), (b, n, ret_w)
            ocf, ocb, s_ctx = retention(qc.reshape(shp_c), kc.reshape(shp_c),
                                        p_c.reshape(b, n_ctx, -1), dec, s0)
            olf, olb, _ = retention(ql.reshape(shp_l), kl.reshape(shp_l),
                                    p_l.reshape(b, n, -1), dec, s_ctx)
            y_l = ret_out(olf.reshape(b * n, ret_w), olb.reshape(b * n, ret_w), p_l)
            h_lat, _ = matmul_resid([y_l], mix_out, 0, h_lat, lat_mod[2], tm=1024, tn=512)
            if not last:
                y_c = ret_out(ocf.reshape(b * n_ctx, ret_w), ocb.reshape(b * n_ctx, ret_w), p_c)
                h_ctx, _ = matmul_resid([y_c], mix_out, 0, h_ctx, ctx_mod[2],
                                        tm=b * n_ctx, tn=512)

        if last:
            h_lat, _, _ = _ffn(h_lat, lat_mod[3:], norm_ffn[i], wg, wu, wd)
        else:
            nj = (i + 1) // 2
            nxt_in, nxt_out = (ev_w_in, ev_w_out) if (i + 1) % 2 == 0 else (ret_w_in, ret_w_out)
            h_lat, cast1, cast2 = _ffn(
                h_lat, lat_mod[3:], norm_ffn[i], wg, wu, wd,
                gate_up_casts=[(ffn_down, i + 1), (ffn_gate, i + 1), (nxt_in, nj)],
                down_casts=[(ffn_up, i + 1), (nxt_out, nj)])
            h_ctx, _, _ = _ffn(h_ctx, ctx_mod[3:], norm_ffn[i], wg, wu, wd)
            (wd, wg, mix_in), (wu, mix_out) = cast1, cast2
    return h_lat.reshape(b, n, d)
```

```python
import functools

import jax
import jax.numpy as jnp
from jax import lax
from jax.experimental import pallas as pl
from jax.experimental.pallas import tpu as pltpu

F32 = jnp.float32
BF16 = jnp.bfloat16

EPS = 1e-6
GRID_W = 64
N_MOD = 6
ROPE_THETA = 10000.0
SGU_CHUNK = 128
SGU_GROUP = 128
HEAD_DIM = 128
KV_HEADS = 4
RET_DIM = 256
RET_CHUNK = 256
LANES = 128
BF16_SUBLANES = 16
LOG2E = 1.4426950408889634
FLASH_UNROLL = 10
FLASH_ROWS = 64
VMEM_LIMIT = 56 * 1024 * 1024


def _cparams(sem):
    return pltpu.CompilerParams(dimension_semantics=sem, vmem_limit_bytes=VMEM_LIMIT)


def _sigmoid(x):
    return 1.0 / (1.0 + jnp.exp(-x))


def _gelu(x):
    return 0.5 * x * (1.0 + jnp.tanh(0.7978845608028654 * (x + 0.044715 * (x * x * x))))


def _dot(a, b):
    return jnp.dot(a, b, preferred_element_type=F32)


def _dot_nt(a, b):
    return lax.dot_general(a, b, (((1,), (1,)), ((), ())), preferred_element_type=F32)


def _dot_tn(a, b):
    return lax.dot_general(a, b, (((0,), (0,)), ((), ())), preferred_element_type=F32)


def _adaln_kernel(cv_ref, down_ref, up_ref, bias_ref, o_ref, hd_ref):
    @pl.when(pl.program_id(1) == 0)
    def _():
        cv = cv_ref[...]
        hd_ref[...] = _dot(cv * _sigmoid(cv), down_ref[...])

    o_ref[...] = _dot(hd_ref[...], up_ref[...]) + bias_ref[...]


def adaln_all(cv, down, up, bias, tn=2048):
    depth, d, rank = down.shape
    nout = up.shape[-1]
    rows = cv.shape[0]
    return pl.pallas_call(
        _adaln_kernel,
        grid=(depth, nout // tn),
        in_specs=[
            pl.BlockSpec((rows, d), lambda l, j: (0, 0)),
            pl.BlockSpec((None, d, rank), lambda l, j: (l, 0, 0)),
            pl.BlockSpec((None, rank, tn), lambda l, j: (l, 0, j)),
            pl.BlockSpec((None, 1, tn), lambda l, j: (l, 0, j)),
        ],
        out_specs=pl.BlockSpec((None, rows, tn), lambda l, j: (l, 0, j)),
        out_shape=jax.ShapeDtypeStruct((depth, rows, nout), F32),
        scratch_shapes=[pltpu.VMEM((rows, rank), F32)],
        compiler_params=_cparams(("arbitrary", "arbitrary")),
        name="adaln",
    )(cv, down, up, bias.reshape(depth, 1, nout))


def _norm_mod_kernel(x_ref, gain_ref, scale_ref, shift_ref, o_ref):
    x = x_ref[...]
    ms = jnp.mean(x * x, axis=-1, keepdims=True)
    xn = x * lax.rsqrt(ms + EPS) * gain_ref[...]
    o_ref[...] = (xn * (1.0 + scale_ref[...]) + shift_ref[...]).astype(o_ref.dtype)


def norm_mod(x, gain, scale, shift, tm=512):
    m, d = x.shape
    nb = scale.shape[0]
    assert (m // tm) % nb == 0
    tpb = (m // tm) // nb
    vec = pl.BlockSpec((None, 1, d), lambda i: (i // tpb, 0, 0))
    return pl.pallas_call(
        _norm_mod_kernel,
        grid=(m // tm,),
        in_specs=[pl.BlockSpec((tm, d), lambda i: (i, 0)),
                  pl.BlockSpec((1, d), lambda i: (0, 0)), vec, vec],
        out_specs=pl.BlockSpec((tm, d), lambda i: (i, 0)),
        out_shape=jax.ShapeDtypeStruct((m, d), BF16),
        compiler_params=_cparams(("parallel",)),
        name="norm_mod",
    )(x, gain.reshape(1, d), scale, shift)


def _mm_store_kernel(a_ref, w_ref, *refs, n_c):
    cast_src, o_ref, cast_dst = refs[:n_c], refs[n_c], refs[n_c + 1:]
    o_ref[...] = _dot(a_ref[...], w_ref[...]).astype(o_ref.dtype)
    _run_casts(cast_src, cast_dst)


def matmul(a, w, layer, out_dtype, tm, tn, casts=()):
    m, k = a.shape
    n = w.shape[2]
    c_in, c_out, c_shapes, c_args = _cast_jobs(casts, m // tm, n // tn)
    outs = pl.pallas_call(
        functools.partial(_mm_store_kernel, n_c=len(casts)),
        grid=(m // tm, n // tn),
        in_specs=[pl.BlockSpec((tm, k), lambda i, j: (i, 0)),
                  pl.BlockSpec((None, k, tn), lambda i, j: (layer, 0, j))] + c_in,
        out_specs=[pl.BlockSpec((tm, tn), lambda i, j: (i, j))] + c_out,
        out_shape=[jax.ShapeDtypeStruct((m, n), out_dtype)] + c_shapes,
        compiler_params=_cparams(("arbitrary", "arbitrary")),
        name="matmul",
    )(a, w, *c_args)
    return outs[0], outs[1:]


def _cast_jobs(casts, steps_i, steps_j):
    in_specs, out_specs, out_shapes, args = [], [], [], []
    for w, layer in casts:
        _, k, c = w.shape
        rows = next(r for r in range(BF16_SUBLANES, k + 1, BF16_SUBLANES)
                    if k % r == 0 and k // r <= steps_i * steps_j)
        last = k // rows - 1
        in_specs.append(pl.BlockSpec(
            (None, rows, c),
            lambda i, j, l=layer, e=last: (l, jnp.minimum(i * steps_j + j, e), 0)))
        out_specs.append(pl.BlockSpec(
            (None, rows, c), lambda i, j, e=last: (0, jnp.minimum(i * steps_j + j, e), 0)))
        out_shapes.append(jax.ShapeDtypeStruct((1, k, c), BF16))
        args.append(w)
    return in_specs, out_specs, out_shapes, args


def _run_casts(src_refs, dst_refs):
    for src, dst in zip(src_refs, dst_refs):
        dst[...] = src[...].astype(dst.dtype)


def _mm_resid_kernel(*refs, n_a, n_c):
    a_refs, w_refs = refs[:n_a], refs[n_a:2 * n_a]
    resid_ref, gate_ref = refs[2 * n_a:2 * n_a + 2]
    cast_src = refs[2 * n_a + 2:2 * n_a + 2 + n_c]
    o_ref = refs[2 * n_a + 2 + n_c]
    cast_dst = refs[2 * n_a + 3 + n_c:]
    acc = _dot(a_refs[0][...], w_refs[0][...])
    for a_ref, w_ref in zip(a_refs[1:], w_refs[1:]):
        acc = acc + _dot(a_ref[...], w_ref[...])
    o_ref[...] = resid_ref[...] + gate_ref[...] * acc
    _run_casts(cast_src, cast_dst)


def matmul_resid(a_list, w, layer, resid, gate, tm, tn, casts=()):
    m, n = resid.shape
    nb = gate.shape[0]
    assert (m // tm) % nb == 0
    tpb = (m // tm) // nb
    n_a = len(a_list)
    c_in, c_out, c_shapes, c_args = _cast_jobs(casts, m // tm, n // tn)
    in_specs, w_args, off = [], [], 0
    for a in a_list:
        in_specs.append(pl.BlockSpec((tm, a.shape[1]), lambda i, j: (i, 0)))
    for a in a_list:
        ka = a.shape[1]
        assert off % ka == 0
        in_specs.append(pl.BlockSpec((None, ka, tn), lambda i, j, r=off // ka: (layer, r, j)))
        w_args.append(w)
        off += ka
    assert off == w.shape[1]
    in_specs += [pl.BlockSpec((tm, tn), lambda i, j: (i, j)),
                 pl.BlockSpec((None, 1, tn), lambda i, j: (i // tpb, 0, j))]
    outs = pl.pallas_call(
        functools.partial(_mm_resid_kernel, n_a=n_a, n_c=len(casts)),
        grid=(m // tm, n // tn),
        in_specs=in_specs + c_in,
        out_specs=[pl.BlockSpec((tm, tn), lambda i, j: (i, j))] + c_out,
        out_shape=[jax.ShapeDtypeStruct((m, n), F32)] + c_shapes,
        compiler_params=_cparams(("arbitrary", "arbitrary")),
        name="matmul_resid",
    )(*a_list, *w_args, resid, gate, *c_args)
    return outs[0], outs[1:]


def _mm_swiglu_kernel(a_ref, wg_ref, wu_ref, *refs, n_c):
    cast_src, o_ref, cast_dst = refs[:n_c], refs[n_c], refs[n_c + 1:]
    a = a_ref[...]
    g = _dot(a, wg_ref[...])
    u = _dot(a, wu_ref[...])
    o_ref[...] = (g * _sigmoid(g) * u).astype(o_ref.dtype)
    _run_casts(cast_src, cast_dst)


def matmul_swiglu(a, wg, wu, layer, tm, tn, casts=()):
    m, k = a.shape
    n = wg.shape[2]
    wspec = pl.BlockSpec((None, k, tn), lambda i, j: (layer, 0, j))
    c_in, c_out, c_shapes, c_args = _cast_jobs(casts, m // tm, n // tn)
    outs = pl.pallas_call(
        functools.partial(_mm_swiglu_kernel, n_c=len(casts)),
        grid=(m // tm, n // tn),
        in_specs=[pl.BlockSpec((tm, k), lambda i, j: (i, 0)), wspec, wspec] + c_in,
        out_specs=[pl.BlockSpec((tm, tn), lambda i, j: (i, j))] + c_out,
        out_shape=[jax.ShapeDtypeStruct((m, n), BF16)] + c_shapes,
        compiler_params=_cparams(("arbitrary", "arbitrary")),
        name="matmul_swiglu",
    )(a, wg, wu, *c_args)
    return outs[0], outs[1:]


def _rope_slab(x, cos, sin_signed, even):
    swapped = jnp.where(even, pltpu.roll(x, LANES - 1, 1), pltpu.roll(x, 1, 1))
    return x * cos + swapped * sin_signed


def _even_lanes(rows):
    return (lax.broadcasted_iota(jnp.int32, (rows, LANES), 1) % 2) == 0


def _sgu_kernel(u_ref, v_ref, gain_ref, w_ref, b_ref, o_ref):
    tm = u_ref.shape[0]
    groups = w_ref.shape[0]
    v = _gelu(v_ref[...].astype(F32))
    ms = jnp.mean(v * v, axis=-1, keepdims=True)
    vn = (v * lax.rsqrt(ms + EPS) * gain_ref[...]).astype(BF16)
    bias = b_ref[...]
    for c in range(tm // SGU_CHUNK):
        rows = slice(c * SGU_CHUNK, (c + 1) * SGU_CHUNK)
        for g in range(groups):
            cols = slice(g * SGU_GROUP, (g + 1) * SGU_GROUP)
            mixed = _dot(w_ref[g], vn[rows, cols]) + bias[:, g:g + 1]
            u = _gelu(u_ref[rows, cols].astype(F32))
            o_ref[rows, cols] = (u * mixed).astype(o_ref.dtype)


def sgu(p, gain, w_s, b_s, tm=256):
    m = p.shape[0]
    groups = w_s.shape[0]
    width = groups * SGU_GROUP
    return pl.pallas_call(
        _sgu_kernel,
        grid=(m // tm,),
        in_specs=[pl.BlockSpec((tm, width), lambda i: (i, 0)),
                  pl.BlockSpec((tm, width), lambda i: (i, 1)),
                  pl.BlockSpec((1, width), lambda i: (0, 0)),
                  pl.BlockSpec((groups, SGU_CHUNK, SGU_CHUNK), lambda i: (0, 0, 0)),
                  pl.BlockSpec((SGU_CHUNK, groups), lambda i: (0, 0))],
        out_specs=pl.BlockSpec((tm, width), lambda i: (i, 0)),
        out_shape=jax.ShapeDtypeStruct((m, width), BF16),
        compiler_params=_cparams(("parallel",)),
        name="sgu",
    )(p, p, gain.reshape(1, width), w_s.astype(BF16), b_s.T)


def _qkv_prep_kernel(q_ref, k_ref, v_ref, qg_ref, kg_ref, cos_ref, sin_ref,
                     qo_ref, ko_ref, vo_ref, *, rope, q_scale):
    tm = q_ref.shape[0]
    even = _even_lanes(tm)
    cos = cos_ref[...]
    sin = sin_ref[...]

    def prep(x, gain):
        ms = jnp.mean(x * x, axis=-1, keepdims=True)
        xn = x * lax.rsqrt(ms + EPS) * gain
        if rope:
            xn = _rope_slab(xn, cos, sin, even)
        return xn

    qg = qg_ref[...]
    kg = kg_ref[...]
    for h in range(q_ref.shape[1] // HEAD_DIM):
        cols = slice(h * HEAD_DIM, (h + 1) * HEAD_DIM)
        qo_ref[:, cols] = (prep(q_ref[:, cols].astype(F32), qg) * q_scale).astype(qo_ref.dtype)
    for h in range(k_ref.shape[1] // HEAD_DIM):
        cols = slice(h * HEAD_DIM, (h + 1) * HEAD_DIM)
        ko_ref[:, cols] = prep(k_ref[:, cols].astype(F32), kg).astype(ko_ref.dtype)
    vo_ref[...] = v_ref[...].astype(vo_ref.dtype)


def qkv_prep(p, q_gain, k_gain, cos, sin, col0, n_q, n_kv, rope, tm=256):
    m = p.shape[0]
    qw, kw = n_q * HEAD_DIM, n_kv * HEAD_DIM
    assert col0 % qw == 0 and (col0 + qw) % kw == 0
    pos_tiles = cos.shape[0] // tm
    tab = pl.BlockSpec((tm, HEAD_DIM), lambda i: (i % pos_tiles, 0))
    kern = functools.partial(_qkv_prep_kernel, rope=rope, q_scale=HEAD_DIM ** -0.5 * LOG2E)
    return pl.pallas_call(
        kern,
        grid=(m // tm,),
        in_specs=[pl.BlockSpec((tm, qw), lambda i: (i, col0 // qw)),
                  pl.BlockSpec((tm, kw), lambda i: (i, (col0 + qw) // kw)),
                  pl.BlockSpec((tm, kw), lambda i: (i, (col0 + qw) // kw + 1)),
                  pl.BlockSpec((1, HEAD_DIM), lambda i: (0, 0)),
                  pl.BlockSpec((1, HEAD_DIM), lambda i: (0, 0)),
                  tab, tab],
        out_specs=[pl.BlockSpec((tm, qw), lambda i: (i, 0)),
                   pl.BlockSpec((tm, kw), lambda i: (i, 0)),
                   pl.BlockSpec((tm, kw), lambda i: (i, 0))],
        out_shape=[jax.ShapeDtypeStruct((m, qw), BF16),
                   jax.ShapeDtypeStruct((m, kw), BF16),
                   jax.ShapeDtypeStruct((m, kw), BF16)],
        compiler_params=_cparams(("parallel",)),
        name="qkv_prep",
    )(p, p, p, q_gain.reshape(1, HEAD_DIM), k_gain.reshape(1, HEAD_DIM), cos, sin)


def _flash_kernel(q_ref, kt_ref, v_ref, o_ref, sa_ref, sb_ref, pa_ref, pb_ref,
                  m_ref, l_ref, acc_ref, alpha_ref, *, tk, group):
    tq = q_ref.shape[0]
    nk = kt_ref.shape[0]
    ncb = tk // LANES
    q = jnp.concatenate(
        [q_ref[:, g * HEAD_DIM:(g + 1) * HEAD_DIM] for g in range(group)], axis=0)
    m_ref[...] = jnp.full(m_ref.shape, -jnp.inf, F32)
    l_ref[...] = jnp.zeros(l_ref.shape, F32)
    acc_ref[...] = jnp.zeros(acc_ref.shape, F32)
    s_bufs = (sa_ref, sb_ref)
    p_bufs = (pa_ref, pb_ref)

    def scores(j, s_ref):
        s_ref[...] = _dot(q, kt_ref[j])

    def pv(j, p_ref):
        off = pl.multiple_of(j * tk, tk)
        acc_ref[...] += _dot(p_ref[...], v_ref[pl.ds(off, tk), :])

    def softmax(s_ref, p_ref):
        for r0 in range(0, group * tq, FLASH_ROWS):
            rows = slice(r0, r0 + FLASH_ROWS)
            blocks = [s_ref[rows, cb * LANES:(cb + 1) * LANES] for cb in range(ncb)]
            mx = blocks[0]
            for blk in blocks[1:]:
                mx = jnp.maximum(mx, blk)
            m_prev = m_ref[rows, :]
            m_new = jnp.maximum(
                m_prev, jnp.broadcast_to(jnp.max(mx, axis=-1, keepdims=True), mx.shape))
            alpha = jnp.exp2(m_prev - m_new)
            psum = None
            for cb, blk in enumerate(blocks):
                pb = jnp.exp2(blk - m_new)
                psum = pb if psum is None else psum + pb
                p_ref[rows, cb * LANES:(cb + 1) * LANES] = pb.astype(BF16)
            l_ref[rows, :] = alpha * l_ref[rows, :] + psum
            alpha_ref[rows, :] = alpha
            m_ref[rows, :] = m_new

    def stage(j, parity, has_next):
        pv(j - 1, p_bufs[1 - parity])
        if has_next:
            scores(j + 1, s_bufs[1 - parity])
        softmax(s_bufs[parity], p_bufs[parity])
        acc_ref[...] = acc_ref[...] * alpha_ref[...]

    scores(0, sa_ref)
    if nk > 1:
        scores(1, sb_ref)
    softmax(sa_ref, pa_ref)

    def body(jj, carry):
        j = FLASH_UNROLL * jj + 1
        for t in range(FLASH_UNROLL):
            stage(j + t, (1 + t) % 2, True)
        return carry

    trips = max(nk - 2, 0) // FLASH_UNROLL
    lax.fori_loop(0, trips, body, 0)
    for j in range(FLASH_UNROLL * trips + 1, nk):
        stage(j, j % 2, j + 1 < nk)
    pv(nk - 1, p_bufs[(nk - 1) % 2])

    out = acc_ref[...] / jnp.sum(l_ref[...], axis=-1, keepdims=True)
    for g in range(group):
        o_ref[:, g * HEAD_DIM:(g + 1) * HEAD_DIM] = out[g * tq:(g + 1) * tq].astype(o_ref.dtype)


def flash_attention(q, k, v, tq, tk):
    b, n, qw = q.shape
    s = k.shape[1]
    kvh = k.shape[2] // HEAD_DIM
    group = qw // HEAD_DIM // kvh
    gw = group * HEAD_DIM
    assert s % tk == 0 and n % tq == 0
    nk = s // tk
    kt = k.reshape(b, nk, tk, kvh, HEAD_DIM).transpose(0, 3, 1, 4, 2)
    return pl.pallas_call(
        functools.partial(_flash_kernel, tk=tk, group=group),
        grid=(b, kvh, n // tq),
        in_specs=[pl.BlockSpec((None, tq, gw), lambda bb, h, i: (bb, i, h)),
                  pl.BlockSpec((None, None, nk, HEAD_DIM, tk), lambda bb, h, i: (bb, h, 0, 0, 0)),
                  pl.BlockSpec((None, s, HEAD_DIM), lambda bb, h, i: (bb, 0, h))],
        out_specs=pl.BlockSpec((None, tq, gw), lambda bb, h, i: (bb, i, h)),
        out_shape=jax.ShapeDtypeStruct((b, n, qw), BF16),
        scratch_shapes=[pltpu.VMEM((group * tq, tk), F32),
                        pltpu.VMEM((group * tq, tk), F32),
                        pltpu.VMEM((group * tq, tk), BF16),
                        pltpu.VMEM((group * tq, tk), BF16),
                        pltpu.VMEM((group * tq, LANES), F32),
                        pltpu.VMEM((group * tq, LANES), F32),
                        pltpu.VMEM((group * tq, HEAD_DIM), F32),
                        pltpu.VMEM((group * tq, HEAD_DIM), F32)],
        compiler_params=_cparams(("parallel", "parallel", "arbitrary")),
        name="flash_attention",
    )(q, kt, v)


def _ret_prep_kernel(q_ref, k_ref, cos_ref, sin_ref, qo_ref, ko_ref, *, rope, k_scale):
    tm = q_ref.shape[0]
    even = _even_lanes(tm)
    slabs_per_head = RET_DIM // LANES
    for sl in range(q_ref.shape[1] // LANES):
        cols = slice(sl * LANES, (sl + 1) * LANES)
        tcols = slice((sl % slabs_per_head) * LANES, (sl % slabs_per_head + 1) * LANES)
        q = q_ref[:, cols].astype(F32)
        k = k_ref[:, cols].astype(F32) * k_scale
        if rope:
            cos = cos_ref[:, tcols]
            sin = sin_ref[:, tcols]
            q = _rope_slab(q, cos, sin, even)
            k = _rope_slab(k, cos, sin, even)
        qo_ref[:, cols] = q.astype(qo_ref.dtype)
        ko_ref[:, cols] = k.astype(ko_ref.dtype)


def ret_prep(p, cos, sin, width, rope, tm=256):
    m = p.shape[0]
    pos_tiles = cos.shape[0] // tm
    tab = pl.BlockSpec((tm, RET_DIM), lambda i: (i % pos_tiles, 0))
    out = jax.ShapeDtypeStruct((m, width), BF16)
    ospec = pl.BlockSpec((tm, width), lambda i: (i, 0))
    return pl.pallas_call(
        functools.partial(_ret_prep_kernel, rope=rope, k_scale=RET_DIM ** -0.5),
        grid=(m // tm,),
        in_specs=[pl.BlockSpec((tm, width), lambda i: (i, 0)),
                  pl.BlockSpec((tm, width), lambda i: (i, 1)),
                  tab, tab],
        out_specs=[ospec, ospec],
        out_shape=[out, out],
        compiler_params=_cparams(("parallel",)),
        name="ret_prep",
    )(p, p, cos, sin)


def _ret_kernel(dec_ref, qf_ref, kf_ref, vf_ref, qb_ref, kb_ref, vb_ref, s0_ref,
                of_ref, ob_ref, sfin_ref, st_ref, dm_ref, rd_ref, *, hb):
    c = RET_CHUNK
    s = pl.program_id(2)

    @pl.when(s == 0)
    def _():
        st_ref[...] = s0_ref[...]
        row = lax.broadcasted_iota(jnp.int32, (c, c), 0).astype(F32)
        col = lax.broadcasted_iota(jnp.int32, (c, c), 1).astype(F32)
        for d in range(2):
            for hh in range(hb):
                ld = -jnp.exp(dec_ref[d, hh])
                if d == 0:
                    dist, keep = row - col, row >= col
                    q_pow, k_pow = row + 1.0, (c - 1.0) - row
                else:
                    dist, keep = col - row, col > row
                    q_pow, k_pow = c - row, row
                dm_ref[d, hh] = jnp.where(keep, jnp.exp(ld * jnp.maximum(dist, 0.0)), 0.0)
                rd_ref[d, hh, 0] = jnp.exp(ld * q_pow)
                rd_ref[d, hh, 1] = jnp.exp(ld * k_pow)

    streams = ((qf_ref, kf_ref, vf_ref, of_ref), (qb_ref, kb_ref, vb_ref, ob_ref))
    for d, (q_ref, k_ref, v_ref, o_ref) in enumerate(streams):
        for hh in range(hb):
            cols = slice(hh * RET_DIM, (hh + 1) * RET_DIM)
            q = q_ref[:, cols]
            k = k_ref[:, cols]
            v = v_ref[:, cols]
            state = st_ref[d, hh]
            att = _dot_nt(q, k) * dm_ref[d, hh]
            o = _dot(att.astype(BF16), v) + rd_ref[d, hh, 0] * _dot(q, state.astype(BF16))
            kt = (k.astype(F32) * rd_ref[d, hh, 1]).astype(BF16)
            chunk_decay = jnp.exp(-jnp.exp(dec_ref[d, hh]) * float(c))
            st_ref[d, hh] = state * chunk_decay + _dot_tn(kt, v)
            o_ref[:, cols] = o

    @pl.when(s == pl.num_programs(2) - 1)
    def _():
        sfin_ref[...] = st_ref[...]


def retention(q, k, p, dec, s0, hb=4):
    b, n, w = q.shape
    h = w // RET_DIM
    c = RET_CHUNK
    nc = n // c
    v0 = 2 * (h // hb)
    fw = pl.BlockSpec((None, c, hb * RET_DIM), lambda bb, g, s: (bb, s, g))
    bw = pl.BlockSpec((None, c, hb * RET_DIM), lambda bb, g, s: (bb, nc - 1 - s, g))
    vfw = pl.BlockSpec((None, c, hb * RET_DIM), lambda bb, g, s: (bb, s, v0 + g))
    vbw = pl.BlockSpec((None, c, hb * RET_DIM), lambda bb, g, s: (bb, nc - 1 - s, v0 + g))
    st = pl.BlockSpec((None, 2, hb, RET_DIM, RET_DIM), lambda bb, g, s: (bb, 0, g, 0, 0))
    return pl.pallas_call(
        functools.partial(_ret_kernel, hb=hb),
        grid=(b, h // hb, nc),
        in_specs=[pl.BlockSpec((2, hb, 1, RET_DIM), lambda bb, g, s: (0, g, 0, 0)),
                  fw, fw, vfw, bw, bw, vbw, st],
        out_specs=[fw, bw, st],
        out_shape=[jax.ShapeDtypeStruct((b, n, w), F32),
                   jax.ShapeDtypeStruct((b, n, w), F32),
                   jax.ShapeDtypeStruct(s0.shape, F32)],
        scratch_shapes=[pltpu.VMEM((2, hb, RET_DIM, RET_DIM), F32),
                        pltpu.VMEM((2, hb, c, c), F32),
                        pltpu.VMEM((2, hb, 2, c, RET_DIM), F32)],
        compiler_params=_cparams(("parallel", "parallel", "arbitrary")),
        name="retention",
    )(dec, q, k, p, q, k, p, s0)


def _ret_out_kernel(of_ref, ob_ref, g_ref, o_ref):
    for h in range(of_ref.shape[1] // RET_DIM):
        cols = slice(h * RET_DIM, (h + 1) * RET_DIM)
        o = of_ref[:, cols] + ob_ref[:, cols]
        ms = jnp.mean(o * o, axis=-1, keepdims=True)
        g = g_ref[:, cols].astype(F32)
        o_ref[:, cols] = (g * _sigmoid(g) * (o * lax.rsqrt(ms + EPS))).astype(o_ref.dtype)


def ret_out(o_f, o_b, p, tm=256):
    m, w = o_f.shape
    spec = pl.BlockSpec((tm, w), lambda i: (i, 0))
    return pl.pallas_call(
        _ret_out_kernel,
        grid=(m // tm,),
        in_specs=[spec, spec, pl.BlockSpec((tm, w), lambda i: (i, 3))],
        out_specs=spec,
        out_shape=jax.ShapeDtypeStruct((m, w), BF16),
        compiler_params=_cparams(("parallel",)),
        name="ret_out",
    )(o_f, o_b, p)


def _rope_tables(rows, head_dim):
    row = jnp.repeat(jnp.arange(rows), GRID_W).astype(F32)
    col = jnp.tile(jnp.arange(GRID_W), rows).astype(F32)
    n_freq = head_dim // 4
    inv_freq = ROPE_THETA ** (-jnp.arange(n_freq, dtype=F32) / n_freq)
    ang = jnp.concatenate([row[:, None] * inv_freq, col[:, None] * inv_freq], axis=-1)
    sign = jnp.tile(jnp.array([-1.0, 1.0], F32), head_dim // 2)
    return (jnp.repeat(jnp.cos(ang), 2, axis=-1),
            jnp.repeat(jnp.sin(ang), 2, axis=-1) * sign)


def _row_tile(m, pref):
    return pref if m % pref == 0 else m


def _ffn(h, mods, norm_gain, wg, wu, wd, gate_up_casts=(), down_casts=()):
    sh2, sc2, g2 = mods
    m = h.shape[0]
    hn = norm_mod(h, norm_gain, sc2, sh2)
    act, cast1 = matmul_swiglu(hn, wg, wu, 0, tm=_row_tile(m, 2048), tn=256,
                               casts=gate_up_casts)
    out, cast2 = matmul_resid([act], wd, 0, h, g2, tm=512, tn=512, casts=down_casts)
    return out, cast1, cast2


def kernel(x, c, ctx, c_ctx, norm_mix, norm_ffn, mod_down, mod_up, mod_bias,
           ffn_gate, ffn_up, ffn_down, ev_w_in, ev_w_out, sgu_norm, sgu_w, sgu_b,
           q_norm, k_norm, ret_w_in, ret_w_out, ret_decay_fwd, ret_decay_bwd):
    b, n, d = x.shape
    n_ctx = ctx.shape[1]
    depth = norm_mix.shape[0]
    a_width = sgu_w.shape[1] * SGU_GROUP
    n_q = (ev_w_out.shape[1] - a_width) // HEAD_DIM
    ret_w = ret_w_out.shape[1]
    ret_heads = ret_w // RET_DIM

    cv = jnp.concatenate([c, c_ctx[None], jnp.zeros((8 - b - 1, d), F32)], axis=0)
    mods = adaln_all(cv, mod_down, mod_up, mod_bias)

    cos_a, sin_a = _rope_tables(n // GRID_W, HEAD_DIM)
    cos_r, sin_r = _rope_tables(n // GRID_W, RET_DIM)

    def first(w):
        return w[:1].astype(BF16)

    mix_in, mix_out = first(ev_w_in), first(ev_w_out)
    wg = wu = wd = None

    h_lat = x.reshape(b * n, d)
    h_ctx = ctx.reshape(b * n_ctx, d)
    for i in range(depth):
        last = i == depth - 1
        j = i // 2
        lat_mod = [mods[i, :b, t * d:(t + 1) * d].reshape(b, 1, d) for t in range(N_MOD)]
        ctx_mod = [mods[i, b:b + 1, t * d:(t + 1) * d].reshape(1, 1, d) for t in range(N_MOD)]
        xn = norm_mod(h_lat, norm_mix[i], lat_mod[1], lat_mod[0])
        cn = norm_mod(h_ctx, norm_mix[i], ctx_mod[1], ctx_mod[0])
        if i % 2 == 0:
            in_casts = [(ffn_gate, 0), (ffn_up, 0)] if i == 0 else []
            out_casts = [(ffn_down, 0)] if i == 0 else []
            p_l, cast_in = matmul(xn, mix_in, 0, BF16, tm=1024, tn=512, casts=in_casts)
            p_c, _ = matmul(cn, mix_in, 0, BF16, tm=b * n_ctx, tn=512)
            ql, kl, vl = qkv_prep(p_l, q_norm[j], k_norm[j], cos_a, sin_a,
                                  2 * a_width, n_q, KV_HEADS, rope=True)
            qc, kc, vc = qkv_prep(p_c, q_norm[j], k_norm[j], cos_a, sin_a,
                                  2 * a_width, n_q, KV_HEADS, rope=False)
            kw = KV_HEADS * HEAD_DIM
            k_all = jnp.concatenate([kc.reshape(b, n_ctx, kw), kl.reshape(b, n, kw)], axis=1)
            v_all = jnp.concatenate([vc.reshape(b, n_ctx, kw), vl.reshape(b, n, kw)], axis=1)
            tk = next(t for t in (384, 512, 256) if (n_ctx + n) % t == 0)
            att_l = flash_attention(ql.reshape(b, n, -1), k_all, v_all, tq=256, tk=tk)
            sgu_l = sgu(p_l, sgu_norm[j], sgu_w[j], sgu_b[j])
            h_lat, cast_out = matmul_resid([sgu_l, att_l.reshape(b * n, -1)], mix_out, 0, h_lat,
                                           lat_mod[2], tm=1024, tn=512, casts=out_casts)
            if i == 0:
                (wg, wu), (wd,) = cast_in, cast_out
            if not last:
                att_c = flash_attention(qc.reshape(b, n_ctx, -1), kc.reshape(b, n_ctx, kw),
                                        vc.reshape(b, n_ctx, kw), tq=n_ctx, tk=n_ctx)
                sgu_c = sgu(p_c, sgu_norm[j], sgu_w[j], sgu_b[j])
                h_ctx, _ = matmul_resid([sgu_c, att_c.reshape(b * n_ctx, -1)], mix_out, 0,
                                        h_ctx, ctx_mod[2], tm=b * n_ctx, tn=512)
        else:
            p_l, _ = matmul(xn, mix_in, 0, BF16, tm=1024, tn=1024)
            p_c, _ = matmul(cn, mix_in, 0, BF16, tm=b * n_ctx, tn=1024)
            ql, kl = ret_prep(p_l, cos_r, sin_r, ret_w, rope=True)
            qc, kc = ret_prep(p_c, cos_r, sin_r, ret_w, rope=False)
            dec = jnp.stack([ret_decay_fwd[j], ret_decay_bwd[j]]).astype(F32)
            dec = jnp.broadcast_to(dec[:, :, None, None], (2, ret_heads, 1, RET_DIM))
            s0 = jnp.zeros((b, 2, ret_heads, RET_DIM, RET_DIM), F32)
            shp_c, shp_l = (b, n_ctx, ret_w), (b, n, ret_w)
            ocf, ocb, s_ctx = retention(qc.reshape(shp_c), kc.reshape(shp_c),
                                        p_c.reshape(b, n_ctx, -1), dec, s0)
            olf, olb, _ = retention(ql.reshape(shp_l), kl.reshape(shp_l),
                                    p_l.reshape(b, n, -1), dec, s_ctx)
            y_l = ret_out(olf.reshape(b * n, ret_w), olb.reshape(b * n, ret_w), p_l)
            h_lat, _ = matmul_resid([y_l], mix_out, 0, h_lat, lat_mod[2], tm=1024, tn=512)
            if not last:
                y_c = ret_out(ocf.reshape(b * n_ctx, ret_w), ocb.reshape(b * n_ctx, ret_w), p_c)
                h_ctx, _ = matmul_resid([y_c], mix_out, 0, h_ctx, ctx_mod[2],
                                        tm=b * n_ctx, tn=512)

        if last:
            h_lat, _, _ = _ffn(h_lat, lat_mod[3:], norm_ffn[i], wg, wu, wd)
        else:
            nj = (i + 1) // 2
            nxt_in, nxt_out = (ev_w_in, ev_w_out) if (i + 1) % 2 == 0 else (ret_w_in, ret_w_out)
            h_lat, cast1, cast2 = _ffn(
                h_lat, lat_mod[3:], norm_ffn[i], wg, wu, wd,
                gate_up_casts=[(ffn_down, i + 1), (ffn_gate, i + 1), (nxt_in, nj)],
                down_casts=[(ffn_up, i + 1), (nxt_out, nj)])
            h_ctx, _, _ = _ffn(h_ctx, ctx_mod[3:], norm_ffn[i], wg, wu, wd)
            (wd, wg, mix_in), (wu, mix_out) = cast1, cast2
    return h_lat.reshape(b, n, d)
```

```python
import functools

import jax
import jax.numpy as jnp
from jax import lax
from jax.experimental import pallas as pl
from jax.experimental.pallas import tpu as pltpu

F32 = jnp.float32
BF16 = jnp.bfloat16

EPS = 1e-6
GRID_W = 64
N_MOD = 6
ROPE_THETA = 10000.0
SGU_CHUNK = 128
SGU_GROUP = 128
HEAD_DIM = 128
KV_HEADS = 4
RET_DIM = 256
RET_CHUNK = 256
LANES = 128
BF16_SUBLANES = 16
LOG2E = 1.4426950408889634
FLASH_UNROLL = 10
FLASH_ROWS = 64
VMEM_LIMIT = 56 * 1024 * 1024

PROJ_TILE = (1024, 512)
RET_IN_TILE = (1024, 1024)
FFN_UP_TILE = (2048, 256)
FFN_DOWN_TILE = (512, 512)
NORM_ROWS = 512
FLASH_TQ = 256
FLASH_TK = (384, 512, 256)


def _cparams(sem):
    return pltpu.CompilerParams(dimension_semantics=sem, vmem_limit_bytes=VMEM_LIMIT)


def _sigmoid(x):
    return 1.0 / (1.0 + jnp.exp(-x))


def _gelu(x):
    return 0.5 * x * (1.0 + jnp.tanh(0.7978845608028654 * (x + 0.044715 * (x * x * x))))


def _dot(a, b):
    return jnp.dot(a, b, preferred_element_type=F32)


def _dot_nt(a, b):
    return lax.dot_general(a, b, (((1,), (1,)), ((), ())), preferred_element_type=F32)


def _dot_tn(a, b):
    return lax.dot_general(a, b, (((0,), (0,)), ((), ())), preferred_element_type=F32)


def _adaln_kernel(cv_ref, down_ref, up_ref, bias_ref, o_ref, hd_ref):
    @pl.when(pl.program_id(1) == 0)
    def _():
        cv = cv_ref[...]
        hd_ref[...] = _dot(cv * _sigmoid(cv), down_ref[...])

    o_ref[...] = _dot(hd_ref[...], up_ref[...]) + bias_ref[...]


def adaln_all(cv, down, up, bias, tn=2048):
    depth, d, rank = down.shape
    nout = up.shape[-1]
    rows = cv.shape[0]
    return pl.pallas_call(
        _adaln_kernel,
        grid=(depth, nout // tn),
        in_specs=[
            pl.BlockSpec((rows, d), lambda l, j: (0, 0)),
            pl.BlockSpec((None, d, rank), lambda l, j: (l, 0, 0)),
            pl.BlockSpec((None, rank, tn), lambda l, j: (l, 0, j)),
            pl.BlockSpec((None, 1, tn), lambda l, j: (l, 0, j)),
        ],
        out_specs=pl.BlockSpec((None, rows, tn), lambda l, j: (l, 0, j)),
        out_shape=jax.ShapeDtypeStruct((depth, rows, nout), F32),
        scratch_shapes=[pltpu.VMEM((rows, rank), F32)],
        compiler_params=_cparams(("arbitrary", "arbitrary")),
        name="adaln",
    )(cv, down, up, bias.reshape(depth, 1, nout))


def _norm_mod_kernel(x_ref, gain_ref, scale_ref, shift_ref, o_ref):
    x = x_ref[...]
    ms = jnp.mean(x * x, axis=-1, keepdims=True)
    xn = x * lax.rsqrt(ms + EPS) * gain_ref[...]
    o_ref[...] = (xn * (1.0 + scale_ref[...]) + shift_ref[...]).astype(o_ref.dtype)


def norm_mod(x, gain, scale, shift):
    m, d = x.shape
    tm = _row_tile(m, NORM_ROWS)
    nb = scale.shape[0]
    assert (m // tm) % nb == 0
    tpb = (m // tm) // nb
    vec = pl.BlockSpec((None, 1, d), lambda i: (i // tpb, 0, 0))
    return pl.pallas_call(
        _norm_mod_kernel,
        grid=(m // tm,),
        in_specs=[pl.BlockSpec((tm, d), lambda i: (i, 0)),
                  pl.BlockSpec((1, d), lambda i: (0, 0)), vec, vec],
        out_specs=pl.BlockSpec((tm, d), lambda i: (i, 0)),
        out_shape=jax.ShapeDtypeStruct((m, d), BF16),
        compiler_params=_cparams(("parallel",)),
        name="norm_mod",
    )(x, gain.reshape(1, d), scale, shift)


def _mm_store_kernel(a_ref, w_ref, *refs, n_c):
    cast_src, o_ref, cast_dst = refs[:n_c], refs[n_c], refs[n_c + 1:]
    o_ref[...] = _dot(a_ref[...], w_ref[...]).astype(o_ref.dtype)
    _run_casts(cast_src, cast_dst)


def matmul(a, w, layer, out_dtype, tm, tn, casts=()):
    m, k = a.shape
    n = w.shape[2]
    c_in, c_out, c_shapes, c_args = _cast_jobs(casts, m // tm, n // tn)
    outs = pl.pallas_call(
        functools.partial(_mm_store_kernel, n_c=len(casts)),
        grid=(m // tm, n // tn),
        in_specs=[pl.BlockSpec((tm, k), lambda i, j: (i, 0)),
                  pl.BlockSpec((None, k, tn), lambda i, j: (layer, 0, j))] + c_in,
        out_specs=[pl.BlockSpec((tm, tn), lambda i, j: (i, j))] + c_out,
        out_shape=[jax.ShapeDtypeStruct((m, n), out_dtype)] + c_shapes,
        compiler_params=_cparams(("arbitrary", "arbitrary")),
        name="matmul",
    )(a, w, *c_args)
    return outs[0], outs[1:]


def _cast_jobs(casts, steps_i, steps_j):
    in_specs, out_specs, out_shapes, args = [], [], [], []
    for w, layer in casts:
        _, k, c = w.shape
        rows = next(r for r in range(BF16_SUBLANES, k + 1, BF16_SUBLANES)
                    if k % r == 0 and k // r <= steps_i * steps_j)
        last = k // rows - 1
        in_specs.append(pl.BlockSpec(
            (None, rows, c),
            lambda i, j, l=layer, e=last: (l, jnp.minimum(i * steps_j + j, e), 0)))
        out_specs.append(pl.BlockSpec(
            (None, rows, c), lambda i, j, e=last: (0, jnp.minimum(i * steps_j + j, e), 0)))
        out_shapes.append(jax.ShapeDtypeStruct((1, k, c), BF16))
        args.append(w)
    return in_specs, out_specs, out_shapes, args


def _run_casts(src_refs, dst_refs):
    for src, dst in zip(src_refs, dst_refs):
        dst[...] = src[...].astype(dst.dtype)


def _mm_resid_kernel(*refs, n_a, n_c):
    a_refs, w_refs = refs[:n_a], refs[n_a:2 * n_a]
    resid_ref, gate_ref = refs[2 * n_a:2 * n_a + 2]
    cast_src = refs[2 * n_a + 2:2 * n_a + 2 + n_c]
    o_ref = refs[2 * n_a + 2 + n_c]
    cast_dst = refs[2 * n_a + 3 + n_c:]
    acc = _dot(a_refs[0][...], w_refs[0][...])
    for a_ref, w_ref in zip(a_refs[1:], w_refs[1:]):
        acc = acc + _dot(a_ref[...], w_ref[...])
    o_ref[...] = resid_ref[...] + gate_ref[...] * acc
    _run_casts(cast_src, cast_dst)


def matmul_resid(a_list, w, layer, resid, gate, tm, tn, casts=()):
    m, n = resid.shape
    nb = gate.shape[0]
    assert (m // tm) % nb == 0
    tpb = (m // tm) // nb
    n_a = len(a_list)
    c_in, c_out, c_shapes, c_args = _cast_jobs(casts, m // tm, n // tn)
    in_specs, w_args, off = [], [], 0
    for a in a_list:
        in_specs.append(pl.BlockSpec((tm, a.shape[1]), lambda i, j: (i, 0)))
    for a in a_list:
        ka = a.shape[1]
        assert off % ka == 0
        in_specs.append(pl.BlockSpec((None, ka, tn), lambda i, j, r=off // ka: (layer, r, j)))
        w_args.append(w)
        off += ka
    assert off == w.shape[1]
    in_specs += [pl.BlockSpec((tm, tn), lambda i, j: (i, j)),
                 pl.BlockSpec((None, 1, tn), lambda i, j: (i // tpb, 0, j))]
    outs = pl.pallas_call(
        functools.partial(_mm_resid_kernel, n_a=n_a, n_c=len(casts)),
        grid=(m // tm, n // tn),
        in_specs=in_specs + c_in,
        out_specs=[pl.BlockSpec((tm, tn), lambda i, j: (i, j))] + c_out,
        out_shape=[jax.ShapeDtypeStruct((m, n), F32)] + c_shapes,
        compiler_params=_cparams(("arbitrary", "arbitrary")),
        name="matmul_resid",
    )(*a_list, *w_args, resid, gate, *c_args)
    return outs[0], outs[1:]


def _mm_swiglu_kernel(a_ref, wg_ref, wu_ref, *refs, n_c):
    cast_src, o_ref, cast_dst = refs[:n_c], refs[n_c], refs[n_c + 1:]
    a = a_ref[...]
    g = _dot(a, wg_ref[...])
    u = _dot(a, wu_ref[...])
    o_ref[...] = (g * _sigmoid(g) * u).astype(o_ref.dtype)
    _run_casts(cast_src, cast_dst)


def matmul_swiglu(a, wg, wu, layer, tm, tn, casts=()):
    m, k = a.shape
    n = wg.shape[2]
    wspec = pl.BlockSpec((None, k, tn), lambda i, j: (layer, 0, j))
    c_in, c_out, c_shapes, c_args = _cast_jobs(casts, m // tm, n // tn)
    outs = pl.pallas_call(
        functools.partial(_mm_swiglu_kernel, n_c=len(casts)),
        grid=(m // tm, n // tn),
        in_specs=[pl.BlockSpec((tm, k), lambda i, j: (i, 0)), wspec, wspec] + c_in,
        out_specs=[pl.BlockSpec((tm, tn), lambda i, j: (i, j))] + c_out,
        out_shape=[jax.ShapeDtypeStruct((m, n), BF16)] + c_shapes,
        compiler_params=_cparams(("arbitrary", "arbitrary")),
        name="matmul_swiglu",
    )(a, wg, wu, *c_args)
    return outs[0], outs[1:]


def _rope_slab(x, cos, sin_signed, even):
    swapped = jnp.where(even, pltpu.roll(x, LANES - 1, 1), pltpu.roll(x, 1, 1))
    return x * cos + swapped * sin_signed


def _even_lanes(rows):
    return (lax.broadcasted_iota(jnp.int32, (rows, LANES), 1) % 2) == 0


def _sgu_kernel(u_ref, v_ref, gain_ref, w_ref, b_ref, o_ref):
    tm = u_ref.shape[0]
    groups = w_ref.shape[0]
    v = _gelu(v_ref[...].astype(F32))
    ms = jnp.mean(v * v, axis=-1, keepdims=True)
    vn = (v * lax.rsqrt(ms + EPS) * gain_ref[...]).astype(BF16)
    bias = b_ref[...]
    for c in range(tm // SGU_CHUNK):
        rows = slice(c * SGU_CHUNK, (c + 1) * SGU_CHUNK)
        for g in range(groups):
            cols = slice(g * SGU_GROUP, (g + 1) * SGU_GROUP)
            mixed = _dot(w_ref[g], vn[rows, cols]) + bias[:, g:g + 1]
            u = _gelu(u_ref[rows, cols].astype(F32))
            o_ref[rows, cols] = (u * mixed).astype(o_ref.dtype)


def sgu(p, gain, w_s, b_s, tm=256):
    m = p.shape[0]
    groups = w_s.shape[0]
    width = groups * SGU_GROUP
    return pl.pallas_call(
        _sgu_kernel,
        grid=(m // tm,),
        in_specs=[pl.BlockSpec((tm, width), lambda i: (i, 0)),
                  pl.BlockSpec((tm, width), lambda i: (i, 1)),
                  pl.BlockSpec((1, width), lambda i: (0, 0)),
                  pl.BlockSpec((groups, SGU_CHUNK, SGU_CHUNK), lambda i: (0, 0, 0)),
                  pl.BlockSpec((SGU_CHUNK, groups), lambda i: (0, 0))],
        out_specs=pl.BlockSpec((tm, width), lambda i: (i, 0)),
        out_shape=jax.ShapeDtypeStruct((m, width), BF16),
        compiler_params=_cparams(("parallel",)),
        name="sgu",
    )(p, p, gain.reshape(1, width), w_s.astype(BF16), b_s.T)


def _qkv_prep_kernel(q_ref, k_ref, v_ref, qg_ref, kg_ref, cos_ref, sin_ref,
                     qo_ref, ko_ref, vo_ref, *, rope, q_scale):
    tm = q_ref.shape[0]
    even = _even_lanes(tm)
    cos = cos_ref[...]
    sin = sin_ref[...]

    def prep(x, gain):
        ms = jnp.mean(x * x, axis=-1, keepdims=True)
        xn = x * lax.rsqrt(ms + EPS) * gain
        if rope:
            xn = _rope_slab(xn, cos, sin, even)
        return xn

    qg = qg_ref[...]
    kg = kg_ref[...]
    for h in range(q_ref.shape[1] // HEAD_DIM):
        cols = slice(h * HEAD_DIM, (h + 1) * HEAD_DIM)
        qo_ref[:, cols] = (prep(q_ref[:, cols].astype(F32), qg) * q_scale).astype(qo_ref.dtype)
    for h in range(k_ref.shape[1] // HEAD_DIM):
        cols = slice(h * HEAD_DIM, (h + 1) * HEAD_DIM)
        ko_ref[:, cols] = prep(k_ref[:, cols].astype(F32), kg).astype(ko_ref.dtype)
    vo_ref[...] = v_ref[...].astype(vo_ref.dtype)


def qkv_prep(p, q_gain, k_gain, cos, sin, col0, n_q, n_kv, rope, tm=256):
    m = p.shape[0]
    qw, kw = n_q * HEAD_DIM, n_kv * HEAD_DIM
    assert col0 % qw == 0 and (col0 + qw) % kw == 0
    pos_tiles = cos.shape[0] // tm
    tab = pl.BlockSpec((tm, HEAD_DIM), lambda i: (i % pos_tiles, 0))
    kern = functools.partial(_qkv_prep_kernel, rope=rope, q_scale=HEAD_DIM ** -0.5 * LOG2E)
    return pl.pallas_call(
        kern,
        grid=(m // tm,),
        in_specs=[pl.BlockSpec((tm, qw), lambda i: (i, col0 // qw)),
                  pl.BlockSpec((tm, kw), lambda i: (i, (col0 + qw) // kw)),
                  pl.BlockSpec((tm, kw), lambda i: (i, (col0 + qw) // kw + 1)),
                  pl.BlockSpec((1, HEAD_DIM), lambda i: (0, 0)),
                  pl.BlockSpec((1, HEAD_DIM), lambda i: (0, 0)),
                  tab, tab],
        out_specs=[pl.BlockSpec((tm, qw), lambda i: (i, 0)),
                   pl.BlockSpec((tm, kw), lambda i: (i, 0)),
                   pl.BlockSpec((tm, kw), lambda i: (i, 0))],
        out_shape=[jax.ShapeDtypeStruct((m, qw), BF16),
                   jax.ShapeDtypeStruct((m, kw), BF16),
                   jax.ShapeDtypeStruct((m, kw), BF16)],
        compiler_params=_cparams(("parallel",)),
        name="qkv_prep",
    )(p, p, p, q_gain.reshape(1, HEAD_DIM), k_gain.reshape(1, HEAD_DIM), cos, sin)


def _flash_kernel(q_ref, kt_ref, v_ref, o_ref, sa_ref, sb_ref, pa_ref, pb_ref,
                  m_ref, l_ref, acc_ref, alpha_ref, *, tk, group):
    tq = q_ref.shape[0]
    nk = kt_ref.shape[0]
    ncb = tk // LANES
    q = jnp.concatenate(
        [q_ref[:, g * HEAD_DIM:(g + 1) * HEAD_DIM] for g in range(group)], axis=0)
    m_ref[...] = jnp.full(m_ref.shape, -jnp.inf, F32)
    l_ref[...] = jnp.zeros(l_ref.shape, F32)
    acc_ref[...] = jnp.zeros(acc_ref.shape, F32)
    s_bufs = (sa_ref, sb_ref)
    p_bufs = (pa_ref, pb_ref)

    def scores(j, s_ref):
        s_ref[...] = _dot(q, kt_ref[j])

    def pv(j, p_ref):
        off = pl.multiple_of(j * tk, tk)
        acc_ref[...] += _dot(p_ref[...], v_ref[pl.ds(off, tk), :])

    def softmax(s_ref, p_ref):
        for r0 in range(0, group * tq, FLASH_ROWS):
            rows = slice(r0, r0 + FLASH_ROWS)
            blocks = [s_ref[rows, cb * LANES:(cb + 1) * LANES] for cb in range(ncb)]
            mx = blocks[0]
            for blk in blocks[1:]:
                mx = jnp.maximum(mx, blk)
            m_prev = m_ref[rows, :]
            m_new = jnp.maximum(
                m_prev, jnp.broadcast_to(jnp.max(mx, axis=-1, keepdims=True), mx.shape))
            alpha = jnp.exp2(m_prev - m_new)
            psum = None
            for cb, blk in enumerate(blocks):
                pb = jnp.exp2(blk - m_new)
                psum = pb if psum is None else psum + pb
                p_ref[rows, cb * LANES:(cb + 1) * LANES] = pb.astype(BF16)
            l_ref[rows, :] = alpha * l_ref[rows, :] + psum
            alpha_ref[rows, :] = alpha
            m_ref[rows, :] = m_new

    def stage(j, parity, has_next):
        pv(j - 1, p_bufs[1 - parity])
        if has_next:
            scores(j + 1, s_bufs[1 - parity])
        softmax(s_bufs[parity], p_bufs[parity])
        acc_ref[...] = acc_ref[...] * alpha_ref[...]

    scores(0, sa_ref)
    if nk > 1:
        scores(1, sb_ref)
    softmax(sa_ref, pa_ref)

    def body(jj, carry):
        j = FLASH_UNROLL * jj + 1
        for t in range(FLASH_UNROLL):
            stage(j + t, (1 + t) % 2, True)
        return carry

    trips = max(nk - 2, 0) // FLASH_UNROLL
    lax.fori_loop(0, trips, body, 0)
    for j in range(FLASH_UNROLL * trips + 1, nk):
        stage(j, j % 2, j + 1 < nk)
    pv(nk - 1, p_bufs[(nk - 1) % 2])

    out = acc_ref[...] / jnp.sum(l_ref[...], axis=-1, keepdims=True)
    for g in range(group):
        o_ref[:, g * HEAD_DIM:(g + 1) * HEAD_DIM] = out[g * tq:(g + 1) * tq].astype(o_ref.dtype)


def flash_attention(q, k, v, tq, tk):
    b, n, qw = q.shape
    s = k.shape[1]
    kvh = k.shape[2] // HEAD_DIM
    group = qw // HEAD_DIM // kvh
    gw = group * HEAD_DIM
    assert s % tk == 0 and n % tq == 0
    nk = s // tk
    kt = k.reshape(b, nk, tk, kvh, HEAD_DIM).transpose(0, 3, 1, 4, 2)
    return pl.pallas_call(
        functools.partial(_flash_kernel, tk=tk, group=group),
        grid=(b, kvh, n // tq),
        in_specs=[pl.BlockSpec((None, tq, gw), lambda bb, h, i: (bb, i, h)),
                  pl.BlockSpec((None, None, nk, HEAD_DIM, tk), lambda bb, h, i: (bb, h, 0, 0, 0)),
                  pl.BlockSpec((None, s, HEAD_DIM), lambda bb, h, i: (bb, 0, h))],
        out_specs=pl.BlockSpec((None, tq, gw), lambda bb, h, i: (bb, i, h)),
        out_shape=jax.ShapeDtypeStruct((b, n, qw), BF16),
        scratch_shapes=[pltpu.VMEM((group * tq, tk), F32),
                        pltpu.VMEM((group * tq, tk), F32),
                        pltpu.VMEM((group * tq, tk), BF16),
                        pltpu.VMEM((group * tq, tk), BF16),
                        pltpu.VMEM((group * tq, LANES), F32),
                        pltpu.VMEM((group * tq, LANES), F32),
                        pltpu.VMEM((group * tq, HEAD_DIM), F32),
                        pltpu.VMEM((group * tq, HEAD_DIM), F32)],
        compiler_params=_cparams(("parallel", "parallel", "arbitrary")),
        name="flash_attention",
    )(q, kt, v)


def _ret_prep_kernel(q_ref, k_ref, cos_ref, sin_ref, qo_ref, ko_ref, *, rope, k_scale):
    tm = q_ref.shape[0]
    even = _even_lanes(tm)
    slabs_per_head = RET_DIM // LANES
    for sl in range(q_ref.shape[1] // LANES):
        cols = slice(sl * LANES, (sl + 1) * LANES)
        tcols = slice((sl % slabs_per_head) * LANES, (sl % slabs_per_head + 1) * LANES)
        q = q_ref[:, cols].astype(F32)
        k = k_ref[:, cols].astype(F32) * k_scale
        if rope:
            cos = cos_ref[:, tcols]
            sin = sin_ref[:, tcols]
            q = _rope_slab(q, cos, sin, even)
            k = _rope_slab(k, cos, sin, even)
        qo_ref[:, cols] = q.astype(qo_ref.dtype)
        ko_ref[:, cols] = k.astype(ko_ref.dtype)


def ret_prep(p, cos, sin, width, rope, tm=256):
    m = p.shape[0]
    pos_tiles = cos.shape[0] // tm
    tab = pl.BlockSpec((tm, RET_DIM), lambda i: (i % pos_tiles, 0))
    out = jax.ShapeDtypeStruct((m, width), BF16)
    ospec = pl.BlockSpec((tm, width), lambda i: (i, 0))
    return pl.pallas_call(
        functools.partial(_ret_prep_kernel, rope=rope, k_scale=RET_DIM ** -0.5),
        grid=(m // tm,),
        in_specs=[pl.BlockSpec((tm, width), lambda i: (i, 0)),
                  pl.BlockSpec((tm, width), lambda i: (i, 1)),
                  tab, tab],
        out_specs=[ospec, ospec],
        out_shape=[out, out],
        compiler_params=_cparams(("parallel",)),
        name="ret_prep",
    )(p, p, cos, sin)


def _ret_kernel(dec_ref, qf_ref, kf_ref, vf_ref, qb_ref, kb_ref, vb_ref, s0_ref,
                of_ref, ob_ref, sfin_ref, st_ref, dm_ref, rd_ref, *, hb):
    c = RET_CHUNK
    s = pl.program_id(2)

    @pl.when(s == 0)
    def _():
        st_ref[...] = s0_ref[...]
        row = lax.broadcasted_iota(jnp.int32, (c, c), 0).astype(F32)
        col = lax.broadcasted_iota(jnp.int32, (c, c), 1).astype(F32)
        for d in range(2):
            for hh in range(hb):
                ld = -jnp.exp(dec_ref[d, hh])
                if d == 0:
                    dist, keep = row - col, row >= col
                    q_pow, k_pow = row + 1.0, (c - 1.0) - row
                else:
                    dist, keep = col - row, col > row
                    q_pow, k_pow = c - row, row
                dm_ref[d, hh] = jnp.where(keep, jnp.exp(ld * jnp.maximum(dist, 0.0)), 0.0)
                rd_ref[d, hh, 0] = jnp.exp(ld * q_pow)
                rd_ref[d, hh, 1] = jnp.exp(ld * k_pow)

    streams = ((qf_ref, kf_ref, vf_ref, of_ref), (qb_ref, kb_ref, vb_ref, ob_ref))
    for d, (q_ref, k_ref, v_ref, o_ref) in enumerate(streams):
        for hh in range(hb):
            cols = slice(hh * RET_DIM, (hh + 1) * RET_DIM)
            q = q_ref[:, cols]
            k = k_ref[:, cols]
            v = v_ref[:, cols]
            state = st_ref[d, hh]
            att = _dot_nt(q, k) * dm_ref[d, hh]
            o = _dot(att.astype(BF16), v) + rd_ref[d, hh, 0] * _dot(q, state.astype(BF16))
            kt = (k.astype(F32) * rd_ref[d, hh, 1]).astype(BF16)
            chunk_decay = jnp.exp(-jnp.exp(dec_ref[d, hh]) * float(c))
            st_ref[d, hh] = state * chunk_decay + _dot_tn(kt, v)
            o_ref[:, cols] = o.astype(o_ref.dtype)

    @pl.when(s == pl.num_programs(2) - 1)
    def _():
        sfin_ref[...] = st_ref[...]


def retention(q, k, p, dec, s0, hb=4):
    b, n, w = q.shape
    h = w // RET_DIM
    c = RET_CHUNK
    nc = n // c
    v0 = 2 * (h // hb)
    fw = pl.BlockSpec((None, c, hb * RET_DIM), lambda bb, g, s: (bb, s, g))
    bw = pl.BlockSpec((None, c, hb * RET_DIM), lambda bb, g, s: (bb, nc - 1 - s, g))
    vfw = pl.BlockSpec((None, c, hb * RET_DIM), lambda bb, g, s: (bb, s, v0 + g))
    vbw = pl.BlockSpec((None, c, hb * RET_DIM), lambda bb, g, s: (bb, nc - 1 - s, v0 + g))
    st = pl.BlockSpec((None, 2, hb, RET_DIM, RET_DIM), lambda bb, g, s: (bb, 0, g, 0, 0))
    return pl.pallas_call(
        functools.partial(_ret_kernel, hb=hb),
        grid=(b, h // hb, nc),
        in_specs=[pl.BlockSpec((2, hb, 1, RET_DIM), lambda bb, g, s: (0, g, 0, 0)),
                  fw, fw, vfw, bw, bw, vbw, st],
        out_specs=[fw, bw, st],
        out_shape=[jax.ShapeDtypeStruct((b, n, w), BF16),
                   jax.ShapeDtypeStruct((b, n, w), BF16),
                   jax.ShapeDtypeStruct(s0.shape, F32)],
        scratch_shapes=[pltpu.VMEM((2, hb, RET_DIM, RET_DIM), F32),
                        pltpu.VMEM((2, hb, c, c), F32),
                        pltpu.VMEM((2, hb, 2, c, RET_DIM), F32)],
        compiler_params=_cparams(("parallel", "parallel", "arbitrary")),
        name="retention",
    )(dec, q, k, p, q, k, p, s0)


def _ret_out_kernel(of_ref, ob_ref, g_ref, o_ref):
    for h in range(of_ref.shape[1] // RET_DIM):
        cols = slice(h * RET_DIM, (h + 1) * RET_DIM)
        o = of_ref[:, cols].astype(F32) + ob_ref[:, cols].astype(F32)
        ms = jnp.mean(o * o, axis=-1, keepdims=True)
        g = g_ref[:, cols].astype(F32)
        o_ref[:, cols] = (g * _sigmoid(g) * (o * lax.rsqrt(ms + EPS))).astype(o_ref.dtype)


def ret_out(o_f, o_b, p, tm=256):
    m, w = o_f.shape
    spec = pl.BlockSpec((tm, w), lambda i: (i, 0))
    return pl.pallas_call(
        _ret_out_kernel,
        grid=(m // tm,),
        in_specs=[spec, spec, pl.BlockSpec((tm, w), lambda i: (i, 3))],
        out_specs=spec,
        out_shape=jax.ShapeDtypeStruct((m, w), BF16),
        compiler_params=_cparams(("parallel",)),
        name="ret_out",
    )(o_f, o_b, p)


def _rope_tables(rows, head_dim):
    row = jnp.repeat(jnp.arange(rows), GRID_W).astype(F32)
    col = jnp.tile(jnp.arange(GRID_W), rows).astype(F32)
    n_freq = head_dim // 4
    inv_freq = ROPE_THETA ** (-jnp.arange(n_freq, dtype=F32) / n_freq)
    ang = jnp.concatenate([row[:, None] * inv_freq, col[:, None] * inv_freq], axis=-1)
    sign = jnp.tile(jnp.array([-1.0, 1.0], F32), head_dim // 2)
    return (jnp.repeat(jnp.cos(ang), 2, axis=-1),
            jnp.repeat(jnp.sin(ang), 2, axis=-1) * sign)


def _row_tile(m, pref):
    return pref if m % pref == 0 else m


def _ffn(h, mods, norm_gain, wg, wu, wd, gate_up_casts=(), down_casts=()):
    sh2, sc2, g2 = mods
    m = h.shape[0]
    hn = norm_mod(h, norm_gain, sc2, sh2)
    act, cast1 = matmul_swiglu(hn, wg, wu, 0, tm=_row_tile(m, FFN_UP_TILE[0]),
                               tn=FFN_UP_TILE[1], casts=gate_up_casts)
    out, cast2 = matmul_resid([act], wd, 0, h, g2, tm=FFN_DOWN_TILE[0], tn=FFN_DOWN_TILE[1],
                              casts=down_casts)
    return out, cast1, cast2


def kernel(x, c, ctx, c_ctx, norm_mix, norm_ffn, mod_down, mod_up, mod_bias,
           ffn_gate, ffn_up, ffn_down, ev_w_in, ev_w_out, sgu_norm, sgu_w, sgu_b,
           q_norm, k_norm, ret_w_in, ret_w_out, ret_decay_fwd, ret_decay_bwd):
    b, n, d = x.shape
    n_ctx = ctx.shape[1]
    depth = norm_mix.shape[0]
    a_width = sgu_w.shape[1] * SGU_GROUP
    n_q = (ev_w_out.shape[1] - a_width) // HEAD_DIM
    ret_w = ret_w_out.shape[1]
    ret_heads = ret_w // RET_DIM

    cv = jnp.concatenate([c, c_ctx[None], jnp.zeros((8 - b - 1, d), F32)], axis=0)
    mods = adaln_all(cv, mod_down, mod_up, mod_bias)

    cos_a, sin_a = _rope_tables(n // GRID_W, HEAD_DIM)
    cos_r, sin_r = _rope_tables(n // GRID_W, RET_DIM)

    def first(w):
        return w[:1].astype(BF16)

    mix_in, mix_out = first(ev_w_in), first(ev_w_out)
    wg = wu = wd = None

    h_lat = x.reshape(b * n, d)
    h_ctx = ctx.reshape(b * n_ctx, d)
    for i in range(depth):
        last = i == depth - 1
        j = i // 2
        lat_mod = [mods[i, :b, t * d:(t + 1) * d].reshape(b, 1, d) for t in range(N_MOD)]
        ctx_mod = [mods[i, b:b + 1, t * d:(t + 1) * d].reshape(1, 1, d) for t in range(N_MOD)]
        xn = norm_mod(h_lat, norm_mix[i], lat_mod[1], lat_mod[0])
        cn = norm_mod(h_ctx, norm_mix[i], ctx_mod[1], ctx_mod[0])
        if i % 2 == 0:
            in_casts = [(ffn_gate, 0), (ffn_up, 0)] if i == 0 else []
            out_casts = [(ffn_down, 0)] if i == 0 else []
            ptm, ptn = PROJ_TILE
            p_l, cast_in = matmul(xn, mix_in, 0, BF16, tm=ptm, tn=ptn, casts=in_casts)
            p_c, _ = matmul(cn, mix_in, 0, BF16, tm=b * n_ctx, tn=ptn)
            ql, kl, vl = qkv_prep(p_l, q_norm[j], k_norm[j], cos_a, sin_a,
                                  2 * a_width, n_q, KV_HEADS, rope=True)
            qc, kc, vc = qkv_prep(p_c, q_norm[j], k_norm[j], cos_a, sin_a,
                                  2 * a_width, n_q, KV_HEADS, rope=False)
            kw = KV_HEADS * HEAD_DIM
            k_all = jnp.concatenate([kc.reshape(b, n_ctx, kw), kl.reshape(b, n, kw)], axis=1)
            v_all = jnp.concatenate([vc.reshape(b, n_ctx, kw), vl.reshape(b, n, kw)], axis=1)
            tk = next(t for t in FLASH_TK if (n_ctx + n) % t == 0)
            att_l = flash_attention(ql.reshape(b, n, -1), k_all, v_all, tq=FLASH_TQ, tk=tk)
            sgu_l = sgu(p_l, sgu_norm[j], sgu_w[j], sgu_b[j])
            h_lat, cast_out = matmul_resid([sgu_l, att_l.reshape(b * n, -1)], mix_out, 0, h_lat,
                                           lat_mod[2], tm=ptm, tn=ptn, casts=out_casts)
            if i == 0:
                (wg, wu), (wd,) = cast_in, cast_out
            if not last:
                att_c = flash_attention(qc.reshape(b, n_ctx, -1), kc.reshape(b, n_ctx, kw),
                                        vc.reshape(b, n_ctx, kw), tq=n_ctx, tk=n_ctx)
                sgu_c = sgu(p_c, sgu_norm[j], sgu_w[j], sgu_b[j])
                h_ctx, _ = matmul_resid([sgu_c, att_c.reshape(b * n_ctx, -1)], mix_out, 0,
                                        h_ctx, ctx_mod[2], tm=b * n_ctx, tn=ptn)
        else:
            ptm, ptn = PROJ_TILE
            p_l, _ = matmul(xn, mix_in, 0, BF16, tm=RET_IN_TILE[0], tn=RET_IN_TILE[1])
            p_c, _ = matmul(cn, mix_in, 0, BF16, tm=b * n_ctx, tn=RET_IN_TILE[1])
            ql, kl = ret_prep(p_l, cos_r, sin_r, ret_w, rope=True)
            qc, kc = ret_prep(p_c, cos_r, sin_r, ret_w, rope=False)
            dec = jnp.stack([ret_decay_fwd[j], ret_decay_bwd[j]]).astype(F32)
            dec = jnp.broadcast_to(dec[:, :, None, None], (2, ret_heads, 1, RET_DIM))
            s0 = jnp.zeros((b, 2, ret_heads, RET_DIM, RET_DIM), F32)
            shp_c, shp_l = (b, n_ctx, ret_w), (b, n, ret_w)
            ocf, ocb, s_ctx = retention(qc.reshape(shp_c), kc.reshape(shp_c),
                                        p_c.reshape(b, n_ctx, -1), dec, s0)
            olf, olb, _ = retention(ql.reshape(shp_l), kl.reshape(shp_l),
                                    p_l.reshape(b, n, -1), dec, s_ctx)
            y_l = ret_out(olf.reshape(b * n, ret_w), olb.reshape(b * n, ret_w), p_l)
            h_lat, _ = matmul_resid([y_l], mix_out, 0, h_lat, lat_mod[2], tm=ptm, tn=ptn)
            if not last:
                y_c = ret_out(ocf.reshape(b * n_ctx, ret_w), ocb.reshape(b * n_ctx, ret_w), p_c)
                h_ctx, _ = matmul_resid([y_c], mix_out, 0, h_ctx, ctx_mod[2],
                                        tm=b * n_ctx, tn=ptn)

        if last:
            h_lat, _, _ = _ffn(h_lat, lat_mod[3:], norm_ffn[i], wg, wu, wd)
        else:
            nj = (i + 1) // 2
            nxt_in, nxt_out = (ev_w_in, ev_w_out) if (i + 1) % 2 == 0 else (ret_w_in, ret_w_out)
            h_lat, cast1, cast2 = _ffn(
                h_lat, lat_mod[3:], norm_ffn[i], wg, wu, wd,
                gate_up_casts=[(ffn_down, i + 1), (ffn_gate, i + 1), (nxt_in, nj)],
                down_casts=[(ffn_up, i + 1), (nxt_out, nj)])
            h_ctx, _, _ = _ffn(h_ctx, ctx_mod[3:], norm_ffn[i], wg, wu, wd)
            (wd, wg, mix_in), (wu, mix_out) = cast1, cast2
    return h_lat.reshape(b, n, d)
```

```python
import functools

import jax
import jax.numpy as jnp
from jax import lax
from jax.experimental import pallas as pl
from jax.experimental.pallas import tpu as pltpu

F32 = jnp.float32
BF16 = jnp.bfloat16

EPS = 1e-6
GRID_W = 64
N_MOD = 6
ROPE_THETA = 10000.0
SGU_CHUNK = 128
SGU_GROUP = 128
HEAD_DIM = 128
KV_HEADS = 4
RET_DIM = 256
RET_CHUNK = 256
LANES = 128
BF16_SUBLANES = 16
LOG2E = 1.4426950408889634
FLASH_UNROLL = 10
FLASH_ROWS = 64
VMEM_LIMIT = 56 * 1024 * 1024

PROJ_TILE = (1024, 512)
RET_IN_TILE = (1024, 1024)
FFN_UP_TILE = (2048, 256)
FFN_DOWN_TILE = (512, 512)
FFN_ROW_SLABS = 2
NORM_ROWS = 512
FLASH_TQ = 256
FLASH_TK = (384, 512, 256)


def _cparams(sem):
    return pltpu.CompilerParams(dimension_semantics=sem, vmem_limit_bytes=VMEM_LIMIT)


def _sigmoid(x):
    return 1.0 / (1.0 + jnp.exp(-x))


def _gelu(x):
    return 0.5 * x * (1.0 + jnp.tanh(0.7978845608028654 * (x + 0.044715 * (x * x * x))))


def _dot(a, b):
    return jnp.dot(a, b, preferred_element_type=F32)


def _dot_nt(a, b):
    return lax.dot_general(a, b, (((1,), (1,)), ((), ())), preferred_element_type=F32)


def _dot_tn(a, b):
    return lax.dot_general(a, b, (((0,), (0,)), ((), ())), preferred_element_type=F32)


def _adaln_kernel(cv_ref, down_ref, up_ref, bias_ref, o_ref, hd_ref):
    @pl.when(pl.program_id(1) == 0)
    def _():
        cv = cv_ref[...]
        hd_ref[...] = _dot(cv * _sigmoid(cv), down_ref[...])

    o_ref[...] = _dot(hd_ref[...], up_ref[...]) + bias_ref[...]


def adaln_all(cv, down, up, bias, tn=2048):
    depth, d, rank = down.shape
    nout = up.shape[-1]
    rows = cv.shape[0]
    return pl.pallas_call(
        _adaln_kernel,
        grid=(depth, nout // tn),
        in_specs=[
            pl.BlockSpec((rows, d), lambda l, j: (0, 0)),
            pl.BlockSpec((None, d, rank), lambda l, j: (l, 0, 0)),
            pl.BlockSpec((None, rank, tn), lambda l, j: (l, 0, j)),
            pl.BlockSpec((None, 1, tn), lambda l, j: (l, 0, j)),
        ],
        out_specs=pl.BlockSpec((None, rows, tn), lambda l, j: (l, 0, j)),
        out_shape=jax.ShapeDtypeStruct((depth, rows, nout), F32),
        scratch_shapes=[pltpu.VMEM((rows, rank), F32)],
        compiler_params=_cparams(("arbitrary", "arbitrary")),
        name="adaln",
    )(cv, down, up, bias.reshape(depth, 1, nout))


def _norm_mod_kernel(x_ref, gain_ref, scale_ref, shift_ref, o_ref):
    x = x_ref[...]
    ms = jnp.mean(x * x, axis=-1, keepdims=True)
    xn = x * lax.rsqrt(ms + EPS) * gain_ref[...]
    o_ref[...] = (xn * (1.0 + scale_ref[...]) + shift_ref[...]).astype(o_ref.dtype)


def norm_mod(x, gain, scale, shift):
    m, d = x.shape
    tm = _row_tile(m, NORM_ROWS)
    nb = scale.shape[0]
    assert (m // tm) % nb == 0
    tpb = (m // tm) // nb
    vec = pl.BlockSpec((None, 1, d), lambda i: (i // tpb, 0, 0))
    return pl.pallas_call(
        _norm_mod_kernel,
        grid=(m // tm,),
        in_specs=[pl.BlockSpec((tm, d), lambda i: (i, 0)),
                  pl.BlockSpec((1, d), lambda i: (0, 0)), vec, vec],
        out_specs=pl.BlockSpec((tm, d), lambda i: (i, 0)),
        out_shape=jax.ShapeDtypeStruct((m, d), BF16),
        compiler_params=_cparams(("parallel",)),
        name="norm_mod",
    )(x, gain.reshape(1, d), scale, shift)


def _mm_store_kernel(a_ref, w_ref, *refs, n_c):
    cast_src, o_ref, cast_dst = refs[:n_c], refs[n_c], refs[n_c + 1:]
    o_ref[...] = _dot(a_ref[...], w_ref[...]).astype(o_ref.dtype)
    _run_casts(cast_src, cast_dst)


def matmul(a, w, layer, out_dtype, tm, tn, casts=()):
    m, k = a.shape
    n = w.shape[2]
    c_in, c_out, c_shapes, c_args = _cast_jobs(casts, m // tm, n // tn)
    outs = pl.pallas_call(
        functools.partial(_mm_store_kernel, n_c=len(casts)),
        grid=(m // tm, n // tn),
        in_specs=[pl.BlockSpec((tm, k), lambda i, j: (i, 0)),
                  pl.BlockSpec((None, k, tn), lambda i, j: (layer, 0, j))] + c_in,
        out_specs=[pl.BlockSpec((tm, tn), lambda i, j: (i, j))] + c_out,
        out_shape=[jax.ShapeDtypeStruct((m, n), out_dtype)] + c_shapes,
        compiler_params=_cparams(("arbitrary", "arbitrary")),
        name="matmul",
    )(a, w, *c_args)
    return outs[0], outs[1:]


def _cast_jobs(casts, steps_i, steps_j):
    in_specs, out_specs, out_shapes, args = [], [], [], []
    for w, layer in casts:
        _, k, c = w.shape
        rows = next(r for r in range(BF16_SUBLANES, k + 1, BF16_SUBLANES)
                    if k % r == 0 and k // r <= steps_i * steps_j)
        last = k // rows - 1
        in_specs.append(pl.BlockSpec(
            (None, rows, c),
            lambda i, j, l=layer, e=last: (l, jnp.minimum(i * steps_j + j, e), 0)))
        out_specs.append(pl.BlockSpec(
            (None, rows, c), lambda i, j, e=last: (0, jnp.minimum(i * steps_j + j, e), 0)))
        out_shapes.append(jax.ShapeDtypeStruct((1, k, c), BF16))
        args.append(w)
    return in_specs, out_specs, out_shapes, args


def _run_casts(src_refs, dst_refs):
    for src, dst in zip(src_refs, dst_refs):
        dst[...] = src[...].astype(dst.dtype)


def _mm_resid_kernel(*refs, n_a, n_c):
    a_refs, w_refs = refs[:n_a], refs[n_a:2 * n_a]
    resid_ref, gate_ref = refs[2 * n_a:2 * n_a + 2]
    cast_src = refs[2 * n_a + 2:2 * n_a + 2 + n_c]
    o_ref = refs[2 * n_a + 2 + n_c]
    cast_dst = refs[2 * n_a + 3 + n_c:]
    acc = _dot(a_refs[0][...], w_refs[0][...])
    for a_ref, w_ref in zip(a_refs[1:], w_refs[1:]):
        acc = acc + _dot(a_ref[...], w_ref[...])
    o_ref[...] = resid_ref[...] + gate_ref[...] * acc
    _run_casts(cast_src, cast_dst)


def matmul_resid(a_list, w, layer, resid, gate, tm, tn, casts=()):
    m, n = resid.shape
    nb = gate.shape[0]
    assert (m // tm) % nb == 0
    tpb = (m // tm) // nb
    n_a = len(a_list)
    c_in, c_out, c_shapes, c_args = _cast_jobs(casts, m // tm, n // tn)
    in_specs, w_args, off = [], [], 0
    for a in a_list:
        in_specs.append(pl.BlockSpec((tm, a.shape[1]), lambda i, j: (i, 0)))
    for a in a_list:
        ka = a.shape[1]
        assert off % ka == 0
        in_specs.append(pl.BlockSpec((None, ka, tn), lambda i, j, r=off // ka: (layer, r, j)))
        w_args.append(w)
        off += ka
    assert off == w.shape[1]
    in_specs += [pl.BlockSpec((tm, tn), lambda i, j: (i, j)),
                 pl.BlockSpec((None, 1, tn), lambda i, j: (i // tpb, 0, j))]
    outs = pl.pallas_call(
        functools.partial(_mm_resid_kernel, n_a=n_a, n_c=len(casts)),
        grid=(m // tm, n // tn),
        in_specs=in_specs + c_in,
        out_specs=[pl.BlockSpec((tm, tn), lambda i, j: (i, j))] + c_out,
        out_shape=[jax.ShapeDtypeStruct((m, n), F32)] + c_shapes,
        compiler_params=_cparams(("arbitrary", "arbitrary")),
        name="matmul_resid",
    )(*a_list, *w_args, resid, gate, *c_args)
    return outs[0], outs[1:]


def _mm_swiglu_kernel(a_ref, wg_ref, wu_ref, *refs, n_c):
    cast_src, o_ref, cast_dst = refs[:n_c], refs[n_c], refs[n_c + 1:]
    slab = a_ref.shape[0] // FFN_ROW_SLABS
    for r0 in range(0, a_ref.shape[0], slab):
        a = a_ref[r0:r0 + slab, :]
        g = _dot(a, wg_ref[...])
        u = _dot(a, wu_ref[...])
        o_ref[r0:r0 + slab, :] = (g * _sigmoid(g) * u).astype(o_ref.dtype)
    _run_casts(cast_src, cast_dst)


def matmul_swiglu(a, wg, wu, layer, tm, tn, casts=()):
    m, k = a.shape
    n = wg.shape[2]
    wspec = pl.BlockSpec((None, k, tn), lambda i, j: (layer, 0, j))
    c_in, c_out, c_shapes, c_args = _cast_jobs(casts, m // tm, n // tn)
    outs = pl.pallas_call(
        functools.partial(_mm_swiglu_kernel, n_c=len(casts)),
        grid=(m // tm, n // tn),
        in_specs=[pl.BlockSpec((tm, k), lambda i, j: (i, 0)), wspec, wspec] + c_in,
        out_specs=[pl.BlockSpec((tm, tn), lambda i, j: (i, j))] + c_out,
        out_shape=[jax.ShapeDtypeStruct((m, n), BF16)] + c_shapes,
        compiler_params=_cparams(("arbitrary", "arbitrary")),
        name="matmul_swiglu",
    )(a, wg, wu, *c_args)
    return outs[0], outs[1:]


def _rope_slab(x, cos, sin_signed, even):
    swapped = jnp.where(even, pltpu.roll(x, LANES - 1, 1), pltpu.roll(x, 1, 1))
    return x * cos + swapped * sin_signed


def _even_lanes(rows):
    return (lax.broadcasted_iota(jnp.int32, (rows, LANES), 1) % 2) == 0


def _sgu_kernel(u_ref, v_ref, gain_ref, w_ref, b_ref, o_ref):
    tm = u_ref.shape[0]
    groups = w_ref.shape[0]
    v = _gelu(v_ref[...].astype(F32))
    ms = jnp.mean(v * v, axis=-1, keepdims=True)
    vn = (v * lax.rsqrt(ms + EPS) * gain_ref[...]).astype(BF16)
    bias = b_ref[...]
    for c in range(tm // SGU_CHUNK):
        rows = slice(c * SGU_CHUNK, (c + 1) * SGU_CHUNK)
        for g in range(groups):
            cols = slice(g * SGU_GROUP, (g + 1) * SGU_GROUP)
            mixed = _dot(w_ref[g], vn[rows, cols]) + bias[:, g:g + 1]
            u = _gelu(u_ref[rows, cols].astype(F32))
            o_ref[rows, cols] = (u * mixed).astype(o_ref.dtype)


def sgu(p, gain, w_s, b_s, tm=256):
    m = p.shape[0]
    groups = w_s.shape[0]
    width = groups * SGU_GROUP
    return pl.pallas_call(
        _sgu_kernel,
        grid=(m // tm,),
        in_specs=[pl.BlockSpec((tm, width), lambda i: (i, 0)),
                  pl.BlockSpec((tm, width), lambda i: (i, 1)),
                  pl.BlockSpec((1, width), lambda i: (0, 0)),
                  pl.BlockSpec((groups, SGU_CHUNK, SGU_CHUNK), lambda i: (0, 0, 0)),
                  pl.BlockSpec((SGU_CHUNK, groups), lambda i: (0, 0))],
        out_specs=pl.BlockSpec((tm, width), lambda i: (i, 0)),
        out_shape=jax.ShapeDtypeStruct((m, width), BF16),
        compiler_params=_cparams(("parallel",)),
        name="sgu",
    )(p, p, gain.reshape(1, width), w_s.astype(BF16), b_s.T)


def _qkv_prep_kernel(q_ref, k_ref, v_ref, qg_ref, kg_ref, cos_ref, sin_ref,
                     qo_ref, ko_ref, vo_ref, *, rope, q_scale):
    tm = q_ref.shape[0]
    even = _even_lanes(tm)
    cos = cos_ref[...]
    sin = sin_ref[...]

    def prep(x, gain):
        ms = jnp.mean(x * x, axis=-1, keepdims=True)
        xn = x * lax.rsqrt(ms + EPS) * gain
        if rope:
            xn = _rope_slab(xn, cos, sin, even)
        return xn

    qg = qg_ref[...]
    kg = kg_ref[...]
    for h in range(q_ref.shape[1] // HEAD_DIM):
        cols = slice(h * HEAD_DIM, (h + 1) * HEAD_DIM)
        qo_ref[:, cols] = (prep(q_ref[:, cols].astype(F32), qg) * q_scale).astype(qo_ref.dtype)
    for h in range(k_ref.shape[1] // HEAD_DIM):
        cols = slice(h * HEAD_DIM, (h + 1) * HEAD_DIM)
        ko_ref[:, cols] = prep(k_ref[:, cols].astype(F32), kg).astype(ko_ref.dtype)
    vo_ref[...] = v_ref[...].astype(vo_ref.dtype)


def qkv_prep(p, q_gain, k_gain, cos, sin, col0, n_q, n_kv, rope, tm=256):
    m = p.shape[0]
    qw, kw = n_q * HEAD_DIM, n_kv * HEAD_DIM
    assert col0 % qw == 0 and (col0 + qw) % kw == 0
    pos_tiles = cos.shape[0] // tm
    tab = pl.BlockSpec((tm, HEAD_DIM), lambda i: (i % pos_tiles, 0))
    kern = functools.partial(_qkv_prep_kernel, rope=rope, q_scale=HEAD_DIM ** -0.5 * LOG2E)
    return pl.pallas_call(
        kern,
        grid=(m // tm,),
        in_specs=[pl.BlockSpec((tm, qw), lambda i: (i, col0 // qw)),
                  pl.BlockSpec((tm, kw), lambda i: (i, (col0 + qw) // kw)),
                  pl.BlockSpec((tm, kw), lambda i: (i, (col0 + qw) // kw + 1)),
                  pl.BlockSpec((1, HEAD_DIM), lambda i: (0, 0)),
                  pl.BlockSpec((1, HEAD_DIM), lambda i: (0, 0)),
                  tab, tab],
        out_specs=[pl.BlockSpec((tm, qw), lambda i: (i, 0)),
                   pl.BlockSpec((tm, kw), lambda i: (i, 0)),
                   pl.BlockSpec((tm, kw), lambda i: (i, 0))],
        out_shape=[jax.ShapeDtypeStruct((m, qw), BF16),
                   jax.ShapeDtypeStruct((m, kw), BF16),
                   jax.ShapeDtypeStruct((m, kw), BF16)],
        compiler_params=_cparams(("parallel",)),
        name="qkv_prep",
    )(p, p, p, q_gain.reshape(1, HEAD_DIM), k_gain.reshape(1, HEAD_DIM), cos, sin)


def _flash_kernel(q_ref, kt_ref, v_ref, o_ref, sa_ref, sb_ref, pa_ref, pb_ref,
                  m_ref, l_ref, acc_ref, alpha_ref, *, tk, group):
    tq = q_ref.shape[0]
    nk = kt_ref.shape[0]
    ncb = tk // LANES
    q = jnp.concatenate(
        [q_ref[:, g * HEAD_DIM:(g + 1) * HEAD_DIM] for g in range(group)], axis=0)
    m_ref[...] = jnp.full(m_ref.shape, -jnp.inf, F32)
    l_ref[...] = jnp.zeros(l_ref.shape, F32)
    acc_ref[...] = jnp.zeros(acc_ref.shape, F32)
    s_bufs = (sa_ref, sb_ref)
    p_bufs = (pa_ref, pb_ref)

    def scores(j, s_ref):
        s_ref[...] = _dot(q, kt_ref[j])

    def pv(j, p_ref):
        off = pl.multiple_of(j * tk, tk)
        acc_ref[...] += _dot(p_ref[...], v_ref[pl.ds(off, tk), :])

    def softmax(s_ref, p_ref):
        for r0 in range(0, group * tq, FLASH_ROWS):
            rows = slice(r0, r0 + FLASH_ROWS)
            blocks = [s_ref[rows, cb * LANES:(cb + 1) * LANES] for cb in range(ncb)]
            mx = blocks[0]
            for blk in blocks[1:]:
                mx = jnp.maximum(mx, blk)
            m_prev = m_ref[rows, :]
            m_new = jnp.maximum(
                m_prev, jnp.broadcast_to(jnp.max(mx, axis=-1, keepdims=True), mx.shape))
            alpha = jnp.exp2(m_prev - m_new)
            psum = None
            for cb, blk in enumerate(blocks):
                pb = jnp.exp2(blk - m_new)
                psum = pb if psum is None else psum + pb
                p_ref[rows, cb * LANES:(cb + 1) * LANES] = pb.astype(BF16)
            l_ref[rows, :] = alpha * l_ref[rows, :] + psum
            alpha_ref[rows, :] = alpha
            m_ref[rows, :] = m_new

    def stage(j, parity, has_next):
        pv(j - 1, p_bufs[1 - parity])
        if has_next:
            scores(j + 1, s_bufs[1 - parity])
        softmax(s_bufs[parity], p_bufs[parity])
        acc_ref[...] = acc_ref[...] * alpha_ref[...]

    scores(0, sa_ref)
    if nk > 1:
        scores(1, sb_ref)
    softmax(sa_ref, pa_ref)

    def body(jj, carry):
        j = FLASH_UNROLL * jj + 1
        for t in range(FLASH_UNROLL):
            stage(j + t, (1 + t) % 2, True)
        return carry

    trips = max(nk - 2, 0) // FLASH_UNROLL
    lax.fori_loop(0, trips, body, 0)
    for j in range(FLASH_UNROLL * trips + 1, nk):
        stage(j, j % 2, j + 1 < nk)
    pv(nk - 1, p_bufs[(nk - 1) % 2])

    out = acc_ref[...] / jnp.sum(l_ref[...], axis=-1, keepdims=True)
    for g in range(group):
        o_ref[:, g * HEAD_DIM:(g + 1) * HEAD_DIM] = out[g * tq:(g + 1) * tq].astype(o_ref.dtype)


def flash_attention(q, k, v, tq, tk):
    b, n, qw = q.shape
    s = k.shape[1]
    kvh = k.shape[2] // HEAD_DIM
    group = qw // HEAD_DIM // kvh
    gw = group * HEAD_DIM
    assert s % tk == 0 and n % tq == 0
    nk = s // tk
    kt = k.reshape(b, nk, tk, kvh, HEAD_DIM).transpose(0, 3, 1, 4, 2)
    return pl.pallas_call(
        functools.partial(_flash_kernel, tk=tk, group=group),
        grid=(b, kvh, n // tq),
        in_specs=[pl.BlockSpec((None, tq, gw), lambda bb, h, i: (bb, i, h)),
                  pl.BlockSpec((None, None, nk, HEAD_DIM, tk), lambda bb, h, i: (bb, h, 0, 0, 0)),
                  pl.BlockSpec((None, s, HEAD_DIM), lambda bb, h, i: (bb, 0, h))],
        out_specs=pl.BlockSpec((None, tq, gw), lambda bb, h, i: (bb, i, h)),
        out_shape=jax.ShapeDtypeStruct((b, n, qw), BF16),
        scratch_shapes=[pltpu.VMEM((group * tq, tk), F32),
                        pltpu.VMEM((group * tq, tk), F32),
                        pltpu.VMEM((group * tq, tk), BF16),
                        pltpu.VMEM((group * tq, tk), BF16),
                        pltpu.VMEM((group * tq, LANES), F32),
                        pltpu.VMEM((group * tq, LANES), F32),
                        pltpu.VMEM((group * tq, HEAD_DIM), F32),
                        pltpu.VMEM((group * tq, HEAD_DIM), F32)],
        compiler_params=_cparams(("parallel", "parallel", "arbitrary")),
        name="flash_attention",
    )(q, kt, v)


def _ret_prep_kernel(q_ref, k_ref, cos_ref, sin_ref, qo_ref, ko_ref, *, rope, k_scale):
    tm = q_ref.shape[0]
    even = _even_lanes(tm)
    slabs_per_head = RET_DIM // LANES
    for sl in range(q_ref.shape[1] // LANES):
        cols = slice(sl * LANES, (sl + 1) * LANES)
        tcols = slice((sl % slabs_per_head) * LANES, (sl % slabs_per_head + 1) * LANES)
        q = q_ref[:, cols].astype(F32)
        k = k_ref[:, cols].astype(F32) * k_scale
        if rope:
            cos = cos_ref[:, tcols]
            sin = sin_ref[:, tcols]
            q = _rope_slab(q, cos, sin, even)
            k = _rope_slab(k, cos, sin, even)
        qo_ref[:, cols] = q.astype(qo_ref.dtype)
        ko_ref[:, cols] = k.astype(ko_ref.dtype)


def ret_prep(p, cos, sin, width, rope, tm=256):
    m = p.shape[0]
    pos_tiles = cos.shape[0] // tm
    tab = pl.BlockSpec((tm, RET_DIM), lambda i: (i % pos_tiles, 0))
    out = jax.ShapeDtypeStruct((m, width), BF16)
    ospec = pl.BlockSpec((tm, width), lambda i: (i, 0))
    return pl.pallas_call(
        functools.partial(_ret_prep_kernel, rope=rope, k_scale=RET_DIM ** -0.5),
        grid=(m // tm,),
        in_specs=[pl.BlockSpec((tm, width), lambda i: (i, 0)),
                  pl.BlockSpec((tm, width), lambda i: (i, 1)),
                  tab, tab],
        out_specs=[ospec, ospec],
        out_shape=[out, out],
        compiler_params=_cparams(("parallel",)),
        name="ret_prep",
    )(p, p, cos, sin)


def _ret_kernel(dec_ref, qf_ref, kf_ref, vf_ref, qb_ref, kb_ref, vb_ref, s0_ref,
                of_ref, ob_ref, sfin_ref, st_ref, dm_ref, rd_ref, *, hb):
    c = RET_CHUNK
    s = pl.program_id(2)

    @pl.when(s == 0)
    def _():
        st_ref[...] = s0_ref[...]
        row = lax.broadcasted_iota(jnp.int32, (c, c), 0).astype(F32)
        col = lax.broadcasted_iota(jnp.int32, (c, c), 1).astype(F32)
        for d in range(2):
            for hh in range(hb):
                ld = -jnp.exp(dec_ref[d, hh])
                if d == 0:
                    dist, keep = row - col, row >= col
                    q_pow, k_pow = row + 1.0, (c - 1.0) - row
                else:
                    dist, keep = col - row, col > row
                    q_pow, k_pow = c - row, row
                dm_ref[d, hh] = jnp.where(keep, jnp.exp(ld * jnp.maximum(dist, 0.0)), 0.0)
                rd_ref[d, hh, 0] = jnp.exp(ld * q_pow)
                rd_ref[d, hh, 1] = jnp.exp(ld * k_pow)

    streams = ((qf_ref, kf_ref, vf_ref, of_ref), (qb_ref, kb_ref, vb_ref, ob_ref))
    for d, (q_ref, k_ref, v_ref, o_ref) in enumerate(streams):
        for hh in range(hb):
            cols = slice(hh * RET_DIM, (hh + 1) * RET_DIM)
            q = q_ref[:, cols]
            k = k_ref[:, cols]
            v = v_ref[:, cols]
            state = st_ref[d, hh]
            att = _dot_nt(q, k) * dm_ref[d, hh]
            o = _dot(att.astype(BF16), v) + rd_ref[d, hh, 0] * _dot(q, state.astype(BF16))
            kt = (k.astype(F32) * rd_ref[d, hh, 1]).astype(BF16)
            chunk_decay = jnp.exp(-jnp.exp(dec_ref[d, hh]) * float(c))
            st_ref[d, hh] = state * chunk_decay + _dot_tn(kt, v)
            o_ref[:, cols] = o.astype(o_ref.dtype)

    @pl.when(s == pl.num_programs(2) - 1)
    def _():
        sfin_ref[...] = st_ref[...]


def retention(q, k, p, dec, s0, hb=4):
    b, n, w = q.shape
    h = w // RET_DIM
    c = RET_CHUNK
    nc = n // c
    v0 = 2 * (h // hb)
    fw = pl.BlockSpec((None, c, hb * RET_DIM), lambda bb, g, s: (bb, s, g))
    bw = pl.BlockSpec((None, c, hb * RET_DIM), lambda bb, g, s: (bb, nc - 1 - s, g))
    vfw = pl.BlockSpec((None, c, hb * RET_DIM), lambda bb, g, s: (bb, s, v0 + g))
    vbw = pl.BlockSpec((None, c, hb * RET_DIM), lambda bb, g, s: (bb, nc - 1 - s, v0 + g))
    st = pl.BlockSpec((None, 2, hb, RET_DIM, RET_DIM), lambda bb, g, s: (bb, 0, g, 0, 0))
    return pl.pallas_call(
        functools.partial(_ret_kernel, hb=hb),
        grid=(b, h // hb, nc),
        in_specs=[pl.BlockSpec((2, hb, 1, RET_DIM), lambda bb, g, s: (0, g, 0, 0)),
                  fw, fw, vfw, bw, bw, vbw, st],
        out_specs=[fw, bw, st],
        out_shape=[jax.ShapeDtypeStruct((b, n, w), BF16),
                   jax.ShapeDtypeStruct((b, n, w), BF16),
                   jax.ShapeDtypeStruct(s0.shape, F32)],
        scratch_shapes=[pltpu.VMEM((2, hb, RET_DIM, RET_DIM), F32),
                        pltpu.VMEM((2, hb, c, c), F32),
                        pltpu.VMEM((2, hb, 2, c, RET_DIM), F32)],
        compiler_params=_cparams(("parallel", "parallel", "arbitrary")),
        name="retention",
    )(dec, q, k, p, q, k, p, s0)


def _ret_out_kernel(of_ref, ob_ref, g_ref, o_ref):
    for h in range(of_ref.shape[1] // RET_DIM):
        cols = slice(h * RET_DIM, (h + 1) * RET_DIM)
        o = of_ref[:, cols].astype(F32) + ob_ref[:, cols].astype(F32)
        ms = jnp.mean(o * o, axis=-1, keepdims=True)
        g = g_ref[:, cols].astype(F32)
        o_ref[:, cols] = (g * _sigmoid(g) * (o * lax.rsqrt(ms + EPS))).astype(o_ref.dtype)


def ret_out(o_f, o_b, p, tm=256):
    m, w = o_f.shape
    spec = pl.BlockSpec((tm, w), lambda i: (i, 0))
    return pl.pallas_call(
        _ret_out_kernel,
        grid=(m // tm,),
        in_specs=[spec, spec, pl.BlockSpec((tm, w), lambda i: (i, 3))],
        out_specs=spec,
        out_shape=jax.ShapeDtypeStruct((m, w), BF16),
        compiler_params=_cparams(("parallel",)),
        name="ret_out",
    )(o_f, o_b, p)


def _rope_tables(rows, head_dim):
    row = jnp.repeat(jnp.arange(rows), GRID_W).astype(F32)
    col = jnp.tile(jnp.arange(GRID_W), rows).astype(F32)
    n_freq = head_dim // 4
    inv_freq = ROPE_THETA ** (-jnp.arange(n_freq, dtype=F32) / n_freq)
    ang = jnp.concatenate([row[:, None] * inv_freq, col[:, None] * inv_freq], axis=-1)
    sign = jnp.tile(jnp.array([-1.0, 1.0], F32), head_dim // 2)
    return (jnp.repeat(jnp.cos(ang), 2, axis=-1),
            jnp.repeat(jnp.sin(ang), 2, axis=-1) * sign)


def _row_tile(m, pref):
    return pref if m % pref == 0 else m


def _ffn(h, mods, norm_gain, wg, wu, wd, gate_up_casts=(), down_casts=()):
    sh2, sc2, g2 = mods
    m = h.shape[0]
    hn = norm_mod(h, norm_gain, sc2, sh2)
    act, cast1 = matmul_swiglu(hn, wg, wu, 0, tm=_row_tile(m, FFN_UP_TILE[0]),
                               tn=FFN_UP_TILE[1], casts=gate_up_casts)
    out, cast2 = matmul_resid([act], wd, 0, h, g2, tm=FFN_DOWN_TILE[0], tn=FFN_DOWN_TILE[1],
                              casts=down_casts)
    return out, cast1, cast2


def kernel(x, c, ctx, c_ctx, norm_mix, norm_ffn, mod_down, mod_up, mod_bias,
           ffn_gate, ffn_up, ffn_down, ev_w_in, ev_w_out, sgu_norm, sgu_w, sgu_b,
           q_norm, k_norm, ret_w_in, ret_w_out, ret_decay_fwd, ret_decay_bwd):
    b, n, d = x.shape
    n_ctx = ctx.shape[1]
    depth = norm_mix.shape[0]
    a_width = sgu_w.shape[1] * SGU_GROUP
    n_q = (ev_w_out.shape[1] - a_width) // HEAD_DIM
    ret_w = ret_w_out.shape[1]
    ret_heads = ret_w // RET_DIM

    cv = jnp.concatenate([c, c_ctx[None], jnp.zeros((8 - b - 1, d), F32)], axis=0)
    mods = adaln_all(cv, mod_down, mod_up, mod_bias)

    cos_a, sin_a = _rope_tables(n // GRID_W, HEAD_DIM)
    cos_r, sin_r = _rope_tables(n // GRID_W, RET_DIM)

    def first(w):
        return w[:1].astype(BF16)

    mix_in, mix_out = first(ev_w_in), first(ev_w_out)
    wg = wu = wd = None

    h_lat = x.reshape(b * n, d)
    h_ctx = ctx.reshape(b * n_ctx, d)
    for i in range(depth):
        last = i == depth - 1
        j = i // 2
        lat_mod = [mods[i, :b, t * d:(t + 1) * d].reshape(b, 1, d) for t in range(N_MOD)]
        ctx_mod = [mods[i, b:b + 1, t * d:(t + 1) * d].reshape(1, 1, d) for t in range(N_MOD)]
        xn = norm_mod(h_lat, norm_mix[i], lat_mod[1], lat_mod[0])
        cn = norm_mod(h_ctx, norm_mix[i], ctx_mod[1], ctx_mod[0])
        if i % 2 == 0:
            in_casts = [(ffn_gate, 0), (ffn_up, 0)] if i == 0 else []
            out_casts = [(ffn_down, 0)] if i == 0 else []
            ptm, ptn = PROJ_TILE
            p_l, cast_in = matmul(xn, mix_in, 0, BF16, tm=ptm, tn=ptn, casts=in_casts)
            p_c, _ = matmul(cn, mix_in, 0, BF16, tm=b * n_ctx, tn=ptn)
            ql, kl, vl = qkv_prep(p_l, q_norm[j], k_norm[j], cos_a, sin_a,
                                  2 * a_width, n_q, KV_HEADS, rope=True)
            qc, kc, vc = qkv_prep(p_c, q_norm[j], k_norm[j], cos_a, sin_a,
                                  2 * a_width, n_q, KV_HEADS, rope=False)
            kw = KV_HEADS * HEAD_DIM
            k_all = jnp.concatenate([kc.reshape(b, n_ctx, kw), kl.reshape(b, n, kw)], axis=1)
            v_all = jnp.concatenate([vc.reshape(b, n_ctx, kw), vl.reshape(b, n, kw)], axis=1)
            tk = next(t for t in FLASH_TK if (n_ctx + n) % t == 0)
            att_l = flash_attention(ql.reshape(b, n, -1), k_all, v_all, tq=FLASH_TQ, tk=tk)
            sgu_l = sgu(p_l, sgu_norm[j], sgu_w[j], sgu_b[j])
            h_lat, cast_out = matmul_resid([sgu_l, att_l.reshape(b * n, -1)], mix_out, 0, h_lat,
                                           lat_mod[2], tm=ptm, tn=ptn, casts=out_casts)
            if i == 0:
                (wg, wu), (wd,) = cast_in, cast_out
            if not last:
                att_c = flash_attention(qc.reshape(b, n_ctx, -1), kc.reshape(b, n_ctx, kw),
                                        vc.reshape(b, n_ctx, kw), tq=n_ctx, tk=n_ctx)
                sgu_c = sgu(p_c, sgu_norm[j], sgu_w[j], sgu_b[j])
                h_ctx, _ = matmul_resid([sgu_c, att_c.reshape(b * n_ctx, -1)], mix_out, 0,
                                        h_ctx, ctx_mod[2], tm=b * n_ctx, tn=ptn)
        else:
            ptm, ptn = PROJ_TILE
            p_l, _ = matmul(xn, mix_in, 0, BF16, tm=RET_IN_TILE[0], tn=RET_IN_TILE[1])
            p_c, _ = matmul(cn, mix_in, 0, BF16, tm=b * n_ctx, tn=RET_IN_TILE[1])
            ql, kl = ret_prep(p_l, cos_r, sin_r, ret_w, rope=True)
            qc, kc = ret_prep(p_c, cos_r, sin_r, ret_w, rope=False)
            dec = jnp.stack([ret_decay_fwd[j], ret_decay_bwd[j]]).astype(F32)
            dec = jnp.broadcast_to(dec[:, :, None, None], (2, ret_heads, 1, RET_DIM))
            s0 = jnp.zeros((b, 2, ret_heads, RET_DIM, RET_DIM), F32)
            shp_c, shp_l = (b, n_ctx, ret_w), (b, n, ret_w)
            ocf, ocb, s_ctx = retention(qc.reshape(shp_c), kc.reshape(shp_c),
                                        p_c.reshape(b, n_ctx, -1), dec, s0)
            olf, olb, _ = retention(ql.reshape(shp_l), kl.reshape(shp_l),
                                    p_l.reshape(b, n, -1), dec, s_ctx)
            y_l = ret_out(olf.reshape(b * n, ret_w), olb.reshape(b * n, ret_w), p_l)
            h_lat, _ = matmul_resid([y_l], mix_out, 0, h_lat, lat_mod[2], tm=ptm, tn=ptn)
            if not last:
                y_c = ret_out(ocf.reshape(b * n_ctx, ret_w), ocb.reshape(b * n_ctx, ret_w), p_c)
                h_ctx, _ = matmul_resid([y_c], mix_out, 0, h_ctx, ctx_mod[2],
                                        tm=b * n_ctx, tn=ptn)

        if last:
            h_lat, _, _ = _ffn(h_lat, lat_mod[3:], norm_ffn[i], wg, wu, wd)
        else:
            nj = (i + 1) // 2
            nxt_in, nxt_out = (ev_w_in, ev_w_out) if (i + 1) % 2 == 0 else (ret_w_in, ret_w_out)
            h_lat, cast1, cast2 = _ffn(
                h_lat, lat_mod[3:], norm_ffn[i], wg, wu, wd,
                gate_up_casts=[(ffn_down, i + 1), (ffn_gate, i + 1), (nxt_in, nj)],
                down_casts=[(ffn_up, i + 1), (nxt_out, nj)])
            h_ctx, _, _ = _ffn(h_ctx, ctx_mod[3:], norm_ffn[i], wg, wu, wd)
            (wd, wg, mix_in), (wu, mix_out) = cast1, cast2
    return h_lat.reshape(b, n, d)
```
